```python
import math
import jax
import jax.numpy as jnp
from jax import lax
import numpy as np

D_MODEL = 2048
BATCH = 1
SEQ = 8192
DEPTH = 1
DEC_BATCH = 32
DEC_SEQ = 1
PAST_LEN = 16384
PAGE_SIZE = 128

RET_HEADS = 8
RET_DK = D_MODEL // 16
RET_DV = 2 * RET_DK
DIFF_HEADS = 8
DIFF_DH = D_MODEL // 16
D_RET_QK = RET_HEADS * RET_DK
D_RET_V = RET_HEADS * RET_DV
D_DIFF_QK = DIFF_HEADS * 2 * DIFF_DH
D_DIFF_V = DIFF_HEADS * 2 * DIFF_DH
D_FF = 4 * D_MODEL
RET_CHUNK = 128
Q_BLOCK = 128
DN_ALPHA = (2.0 * DEPTH) ** 0.25
DN_BETA = (8.0 * DEPTH) ** -0.25
LN_EPS = 1e-5
NORM_EPS = 1e-5
PROJ_SIZES = (D_RET_QK, D_RET_QK, D_RET_V, D_RET_V, D_DIFF_QK, D_DIFF_QK, D_DIFF_V, D_MODEL, D_MODEL)
PROJ_SPLITS = tuple(int(c) for c in np.cumsum(PROJ_SIZES)[:-1])
D_IN = int(sum(PROJ_SIZES))
V_COLUMN_GROUPS = (2, 6)

kernel_name = 'retnet_diffattn_gated_hybrid_step'


def _layer_norm(x, g, b):
    xf = x.astype(jnp.float32)
    mu = jnp.mean(xf, -1, keepdims=True)
    xc = xf - mu
    var = jnp.mean(xc * xc, -1, keepdims=True)
    return (xc * lax.rsqrt(var + LN_EPS) * g.astype(jnp.float32) + b.astype(jnp.float32)).astype(x.dtype)


def _rms_norm(xf, g=None):
    y = xf * lax.rsqrt(jnp.mean(xf * xf, -1, keepdims=True) + NORM_EPS)
    return y if g is None else y * g.astype(jnp.float32)


def _ret_log_gamma():
    return jnp.log1p(-jnp.exp2(-5.0 - jnp.arange(RET_HEADS, dtype=jnp.float32)))


def _alibi_slopes():
    return jnp.exp2(-8.0 * jnp.arange(1, DIFF_HEADS + 1, dtype=jnp.float32) / DIFF_HEADS)


def _project(x, w_in):
    B, L, _ = x.shape
    h = jnp.einsum('bld,de->ble', x, w_in)
    rq, rk, rv, rg, dq, dk, dv, ga, gb = jnp.split(h, PROJ_SPLITS, axis=-1)
    ret = (rq.reshape(B, L, RET_HEADS, RET_DK),
           rk.reshape(B, L, RET_HEADS, RET_DK) * (RET_DK ** -0.5),
           rv.reshape(B, L, RET_HEADS, RET_DV))
    diff = (dq.reshape(B, L, DIFF_HEADS, 2, DIFF_DH),
            dk.reshape(B, L, DIFF_HEADS, 2, DIFF_DH),
            dv.reshape(B, L, DIFF_HEADS, 2 * DIFF_DH))
    return ret, diff, (rg, ga, gb)


def _retention_chunk(q, k, v, state):
    L = q.shape[1]
    lg = _ret_log_gamma()
    pos = jnp.arange(L, dtype=jnp.float32)
    rel = pos[:, None] - pos[None, :]
    decay = jnp.where(rel >= 0, jnp.exp(lg[:, None, None] * jnp.maximum(rel, 0.0)), 0.0)
    s = jnp.einsum('bihd,bjhd->bhij', q, k) * decay
    inner = jnp.einsum('bhij,bjhv->bihv', s, v)
    cross = jnp.einsum('bihd,bhdv->bihv', q, state) * jnp.exp(lg[None, :] * (pos[:, None] + 1.0))[None, :, :, None]
    k_dec = k * jnp.exp(lg[None, :] * (L - 1.0 - pos[:, None]))[None, :, :, None]
    new_state = jnp.exp(lg * L)[None, :, None, None] * state + jnp.einsum('bjhd,bjhv->bhdv', k_dec, v)
    return inner + cross, new_state


def _retention_prompt(q, k, v):
    B, S = q.shape[:2]
    nc = S // RET_CHUNK

    def to_chunks(a):
        return jnp.swapaxes(a.astype(jnp.float32).reshape(B, nc, RET_CHUNK, *a.shape[2:]), 0, 1)

    def step(state, qkv):
        o, state = _retention_chunk(qkv[0], qkv[1], qkv[2], state)
        return state, o

    state0 = jnp.zeros((B, RET_HEADS, RET_DK, RET_DV), jnp.float32)
    state, o = lax.scan(step, state0, (to_chunks(q), to_chunks(k), to_chunks(v)))
    return jnp.swapaxes(o, 0, 1).reshape(B, S, RET_HEADS, RET_DV), state


def _diff_prompt(q, k, v):
    B, S = q.shape[:2]
    nb = S // Q_BLOCK
    slopes = _alibi_slopes()[None, :, None, None, None]
    kf = k.astype(jnp.float32)
    vf = v.astype(jnp.float32)
    qb = jnp.swapaxes((q.astype(jnp.float32) * (DIFF_DH ** -0.5)).reshape(B, nb, Q_BLOCK, DIFF_HEADS, 2, DIFF_DH), 0, 1)
    kpos = jnp.arange(S)

    def block(args):
        qblk, i = args
        qpos = i * Q_BLOCK + jnp.arange(Q_BLOCK)
        dist = (qpos[:, None] - kpos[None, :]).astype(jnp.float32)
        s = jnp.einsum('bqhcd,bkhcd->bhcqk', qblk, kf) - slopes * dist
        s = jnp.where(dist >= 0, s, -jnp.inf)
        p = jax.nn.softmax(s, axis=-1)
        return jnp.einsum('bhcqk,bkhv->bqhcv', p, vf)

    o = lax.map(block, (qb, jnp.arange(nb)))
    return jnp.swapaxes(o, 0, 1).reshape(B, S, DIFF_HEADS, 2, 2 * DIFF_DH)


def _diff_sample(q, k_new, v_new, cache_k, cache_v, layer, page_table):
    Bd, L = q.shape[:2]
    n_pages = page_table.shape[1]
    slopes = _alibi_slopes()[None, :, None, None, None]
    qf = q.astype(jnp.float32) * (DIFF_DH ** -0.5)
    qpos = n_pages * PAGE_SIZE + jnp.arange(L)

    def scores(kblk, kpos):
        dist = (qpos[:, None] - kpos[None, :]).astype(jnp.float32)
        s = jnp.einsum('bqhcd,bkhcd->bhcqk', qf, kblk.astype(jnp.float32)) - slopes * dist
        return s, dist

    def accumulate(carry, s, vblk):
        m, l, acc = carry
        m_new = jnp.maximum(m, jnp.max(s, -1))
        corr = jnp.exp(m - m_new)
        p = jnp.exp(s - m_new[..., None])
        l = l * corr + jnp.sum(p, -1)
        acc = acc * corr[..., None] + jnp.einsum('bhcqk,bkhv->bhcqv', p, vblk.astype(jnp.float32))
        return (m_new, l, acc)

    def page_step(carry, p):
        phys = page_table[:, p]
        kblk = cache_k[layer, phys]
        vblk = cache_v[layer, phys]
        s, _ = scores(kblk, p * PAGE_SIZE + jnp.arange(PAGE_SIZE))
        return accumulate(carry, s, vblk), None

    init = (jnp.full((Bd, DIFF_HEADS, 2, L), -jnp.inf, jnp.float32),
            jnp.zeros((Bd, DIFF_HEADS, 2, L), jnp.float32),
            jnp.zeros((Bd, DIFF_HEADS, 2, L, 2 * DIFF_DH), jnp.float32))
    carry, _ = lax.scan(page_step, init, jnp.arange(n_pages))
    s, dist = scores(k_new, qpos)
    s = jnp.where(dist >= 0, s, -jnp.inf)
    m, l, acc = accumulate(carry, s, v_new)
    return jnp.transpose(acc / l[..., None], (0, 3, 1, 2, 4))


def _diff_lambda(lq1, lk1, lq2, lk2, lam_init):
    f = jnp.float32
    return (jnp.exp(jnp.sum(lq1.astype(f) * lk1.astype(f))) - jnp.exp(jnp.sum(lq2.astype(f) * lk2.astype(f))) + lam_init)


def _finish(x, o_ret, o_dif, gates, lam, lam_init, w_ret_out, w_diff_out, w_out, subln_g,
            ln1_g, ln1_b, ln2_g, ln2_b, w_up, w_down):
    rg, ga, gb = gates
    B, L, _ = x.shape
    dt = x.dtype
    r = _rms_norm(o_ret).reshape(B, L, D_RET_V).astype(dt) * jax.nn.silu(rg)
    a = jnp.einsum('ble,ed->bld', r, w_ret_out)
    od = o_dif[..., 0, :] - lam * o_dif[..., 1, :]
    od = (_rms_norm(od, subln_g) * (1.0 - lam_init)).reshape(B, L, D_DIFF_V).astype(dt)
    b = jnp.einsum('ble,ed->bld', od, w_diff_out)
    merged = jax.nn.sigmoid(ga) * a + jax.nn.sigmoid(gb) * b
    mix = jnp.einsum('bld,de->ble', merged, w_out)
    x1 = _layer_norm(DN_ALPHA * x + mix, ln1_g, ln1_b)
    hid = jnp.square(jax.nn.relu(jnp.einsum('bld,df->blf', x1, w_up)))
    ff = jnp.einsum('blf,fd->bld', hid, w_down)
    return _layer_norm(DN_ALPHA * x1 + ff, ln2_g, ln2_b)


def setup_inputs(seed: int = 0) -> dict:
    key = jax.random.key(seed)
    ks = jax.random.split(key, 24)
    f32 = jnp.float32
    n_pages = PAST_LEN // PAGE_SIZE
    n_used = DEC_BATCH * n_pages
    n_pool = (5 * n_used + 3) // 4
    nrm = lambda k, shape: jax.random.normal(k, shape, f32)
    x_prompt = nrm(ks[0], (BATCH, SEQ, D_MODEL))
    x_sample = nrm(ks[1], (DEC_BATCH, DEC_SEQ, D_MODEL))
    cache_k = nrm(ks[2], (DEPTH, n_pool, PAGE_SIZE, DIFF_HEADS, 2, DIFF_DH))
    cache_v = nrm(ks[3], (DEPTH, n_pool, PAGE_SIZE, DIFF_HEADS, 2 * DIFF_DH)) * DN_BETA
    state_ret = nrm(ks[4], (DEPTH, DEC_BATCH, RET_HEADS, RET_DK, RET_DV)) * 0.25
    page_table = jax.random.permutation(ks[5], n_pool)[:n_used].reshape(DEC_BATCH, n_pages).astype(jnp.int32)
    col_scale = jnp.concatenate([jnp.full((s,), DN_BETA if i in V_COLUMN_GROUPS else 1.0, f32)
                                 for i, s in enumerate(PROJ_SIZES)])
    w_in = nrm(ks[6], (DEPTH, D_MODEL, D_IN)) * (D_MODEL ** -0.5) * col_scale
    w_ret_out = nrm(ks[7], (DEPTH, D_RET_V, D_MODEL)) * (D_RET_V ** -0.5) * DN_BETA
    w_diff_out = nrm(ks[8], (DEPTH, D_DIFF_V, D_MODEL)) * (D_DIFF_V ** -0.5) * DN_BETA
    w_out = nrm(ks[9], (DEPTH, D_MODEL, D_MODEL)) * (D_MODEL ** -0.5) * DN_BETA
    lambda_q1 = nrm(ks[10], (DEPTH, DIFF_DH)) * 0.1
    lambda_k1 = nrm(ks[11], (DEPTH, DIFF_DH)) * 0.1
    lambda_q2 = nrm(ks[12], (DEPTH, DIFF_DH)) * 0.1
    lambda_k2 = nrm(ks[13], (DEPTH, DIFF_DH)) * 0.1
    subln_g = 1.0 + 0.02 * nrm(ks[14], (DEPTH, 2 * DIFF_DH))
    ln1_g = 1.0 + 0.02 * nrm(ks[15], (DEPTH, D_MODEL))
    ln1_b = 0.02 * nrm(ks[16], (DEPTH, D_MODEL))
    ln2_g = 1.0 + 0.02 * nrm(ks[17], (DEPTH, D_MODEL))
    ln2_b = 0.02 * nrm(ks[18], (DEPTH, D_MODEL))
    w_up = nrm(ks[19], (DEPTH, D_MODEL, D_FF)) * (D_MODEL ** -0.5)
    w_down = nrm(ks[20], (DEPTH, D_FF, D_MODEL)) * (D_FF ** -0.5) * DN_BETA
    return {'x_prompt': x_prompt, 'x_sample': x_sample, 'cache_k': cache_k, 'cache_v': cache_v,
            'state_ret': state_ret, 'page_table': page_table, 'w_in': w_in, 'w_ret_out': w_ret_out,
            'w_diff_out': w_diff_out, 'w_out': w_out, 'lambda_q1': lambda_q1, 'lambda_k1': lambda_k1,
            'lambda_q2': lambda_q2, 'lambda_k2': lambda_k2, 'subln_g': subln_g, 'ln1_g': ln1_g,
            'ln1_b': ln1_b, 'ln2_g': ln2_g, 'ln2_b': ln2_b, 'w_up': w_up, 'w_down': w_down}


def reference(x_prompt, x_sample, cache_k, cache_v, state_ret, page_table, w_in, w_ret_out,
              w_diff_out, w_out, lambda_q1, lambda_k1, lambda_q2, lambda_k2, subln_g, ln1_g,
              ln1_b, ln2_g, ln2_b, w_up, w_down):
    xp, xs = x_prompt, x_sample
    kp_list, vp_list, sp_list, ks_list, vs_list, ss_list = [], [], [], [], [], []
    for l in range(DEPTH):
        lam_init = 0.8 - 0.6 * math.exp(-0.3 * l)
        lam = _diff_lambda(lambda_q1[l], lambda_k1[l], lambda_q2[l], lambda_k2[l], lam_init)
        w_in_l = w_in[l]
        post = (w_ret_out[l], w_diff_out[l], w_out[l], subln_g[l], ln1_g[l], ln1_b[l],
                ln2_g[l], ln2_b[l], w_up[l], w_down[l])
        (rq, rk, rv), (dq, dk, dv), gates = _project(xp, w_in_l)
        o_ret, st_p = _retention_prompt(rq, rk, rv)
        o_dif = _diff_prompt(dq, dk, dv)
        xp = _finish(xp, o_ret, o_dif, gates, lam, lam_init, *post)
        kp_list.append(dk)
        vp_list.append(dv)
        sp_list.append(st_p.astype(state_ret.dtype))
        (rq, rk, rv), (dq, dk, dv), gates = _project(xs, w_in_l)
        o_ret, st_s = _retention_chunk(rq.astype(jnp.float32), rk.astype(jnp.float32),
                                       rv.astype(jnp.float32), state_ret[l].astype(jnp.float32))
        o_dif = _diff_sample(dq, dk, dv, cache_k, cache_v, l, page_table)
        xs = _finish(xs, o_ret, o_dif, gates, lam, lam_init, *post)
        ks_list.append(dk)
        vs_list.append(dv)
        ss_list.append(st_s.astype(state_ret.dtype))
    k_prompt = jnp.stack(kp_list)
    v_prompt = jnp.stack(vp_list)
    ret_state_prompt = jnp.stack(sp_list)
    k_sample = jnp.stack(ks_list)
    v_sample = jnp.stack(vs_list)
    ret_state_sample = jnp.stack(ss_list)
    return (xp, xs, k_prompt, v_prompt, ret_state_prompt, k_sample, v_sample, ret_state_sample)
```

```python
import functools
import math

import jax
import jax.numpy as jnp
import numpy as np
from jax import lax
from jax.experimental import pallas as pl
from jax.experimental.pallas import tpu as pltpu

D_MODEL = 2048
N_HEADS = 8
RET_DK = 128
RET_DV = 256
DIFF_DH = 128
DIFF_DV = 256
D_FF = 4 * D_MODEL
PAGE_SIZE = 128
DN_ALPHA = 2.0 ** 0.25
LAM_INIT = 0.8 - 0.6 * math.exp(-0.3 * 0)
LN_EPS = 1e-5
NORM_EPS = 1e-5
RET_KSCALE = RET_DK ** -0.5
DIFF_QSCALE = DIFF_DH ** -0.5

COL_RET_QKV = 0
COL_RET_GATE = 4096
COL_DIFF_Q = 6144
COL_DIFF_K = 8192
COL_DIFF_V = 10240
COL_MERGE_GATES = 12288

V7X_VMEM_LIMIT_BYTES = 56 * 1024 * 1024
LANES = 128
SUBLANES = 8

RET_CHUNK = 256
FLASH_BQ = 1024
FLASH_BK = 512
DEC_PAGES_PER_STEP = 8

K_ROWS = 2 * N_HEADS
V_ROWS = N_HEADS

_NT = (((1,), (1,)), ((), ()))
_TN = (((0,), (0,)), ((), ()))
_BF16 = jnp.bfloat16
_F32 = jnp.float32


def _params(*semantics):
    return pltpu.CompilerParams(dimension_semantics=semantics,
                                vmem_limit_bytes=V7X_VMEM_LIMIT_BYTES)


def _dot(a, b):
    return jnp.dot(a, b, preferred_element_type=_F32)


def _layer_norm(y, g, b):
    mu = jnp.mean(y, -1, keepdims=True)
    yc = y - mu
    var = jnp.mean(yc * yc, -1, keepdims=True)
    return yc * lax.rsqrt(var + LN_EPS) * g + b


def _rms(y):
    return y * lax.rsqrt(jnp.mean(y * y, -1, keepdims=True) + NORM_EPS)


def _silu(g):
    return g * jax.nn.sigmoid(g)


def _diff_lambda(lam_ref):
    v = lam_ref[...]
    a = jnp.sum(v[0:1] * v[1:2], axis=-1, keepdims=True)
    b = jnp.sum(v[2:3] * v[3:4], axis=-1, keepdims=True)
    return jnp.exp(a) - jnp.exp(b) + LAM_INIT


def _sub_ln(o1, o2, lam_ref, g_ref):
    od = o1 - _diff_lambda(lam_ref) * o2
    return _rms(od) * g_ref[...] * (1.0 - LAM_INIT)


def _proj_body(x_ref, w_ref, *o_refs, scale):
    acc = _dot(x_ref[...], w_ref[...].astype(_BF16))
    if scale is not None:
        acc = acc * scale
    for o_ref in o_refs:
        o_ref[...] = acc.astype(o_ref.dtype)


def _proj(x, w, col0, ncols, out_dtypes, scale=None, tn=512):
    m, kd = x.shape
    tm = min(2048, m)
    c0 = col0 // tn
    return pl.pallas_call(
        functools.partial(_proj_body, scale=scale),
        grid=(m // tm, ncols // tn),
        in_specs=[pl.BlockSpec((tm, kd), lambda i, j: (i, 0)),
                  pl.BlockSpec((kd, tn), lambda i, j: (0, c0 + j))],
        out_specs=[pl.BlockSpec((tm, tn), lambda i, j: (i, j)) for _ in out_dtypes],
        out_shape=[jax.ShapeDtypeStruct((m, ncols), dt) for dt in out_dtypes],
        compiler_params=_params("parallel", "arbitrary"),
        name="proj",
    )(x, w)


def _proj_k_body(x_ref, w_ref, cache_ref, o16_ref, *, tm):
    h = pl.program_id(1)
    acc = _dot(x_ref[...], w_ref[...].astype(_BF16))
    o16_ref[...] = acc.astype(_BF16)
    for c in range(2):
        cache_ref[pl.ds(h * 2 + c, tm, stride=K_ROWS), :] = acc[:, c * DIFF_DH:(c + 1) * DIFF_DH]


def _proj_k(x, w, col0, tm=1024):
    m, kd = x.shape
    c0 = col0 // 256
    return pl.pallas_call(
        functools.partial(_proj_k_body, tm=tm),
        grid=(m // tm, N_HEADS),
        in_specs=[pl.BlockSpec((tm, kd), lambda i, h: (i, 0)),
                  pl.BlockSpec((kd, 256), lambda i, h: (0, c0 + h))],
        out_specs=[pl.BlockSpec((tm * K_ROWS, DIFF_DH), lambda i, h: (i, 0)),
                   pl.BlockSpec((tm, 256), lambda i, h: (i, h))],
        out_shape=[jax.ShapeDtypeStruct((m * K_ROWS, DIFF_DH), _F32),
                   jax.ShapeDtypeStruct((m, D_MODEL), _BF16)],
        compiler_params=_params("parallel", "arbitrary"),
        name="proj_k",
    )(x, w)


def _ret_body(lg_ref, q_ref, k_ref, v_ref, g_ref, r_ref, st_ref, state, dec, rowsc, colsc, *, chunk):
    c = pl.program_id(0)

    @pl.when(c == 0)
    def _():
        state[...] = jnp.zeros_like(state)
        i = lax.broadcasted_iota(jnp.int32, (chunk, chunk), 0)
        j = lax.broadcasted_iota(jnp.int32, (chunk, chunk), 1)
        rel = (i - j).astype(_F32)
        pos = lax.broadcasted_iota(jnp.int32, (chunk, 1), 0).astype(_F32)
        for h in range(N_HEADS):
            lg = lg_ref[h]
            dec[h] = jnp.where(rel >= 0, jnp.exp(lg * jnp.maximum(rel, 0.0)), 0.0) * RET_KSCALE
            rowsc[h] = jnp.exp(lg * (pos + 1.0))
            colsc[h] = jnp.exp(lg * (chunk - 1.0 - pos)) * RET_KSCALE

    for h in range(N_HEADS):
        q = q_ref[:, h * RET_DK:(h + 1) * RET_DK]
        k = k_ref[:, h * RET_DK:(h + 1) * RET_DK]
        v = v_ref[:, h * RET_DV:(h + 1) * RET_DV]
        s = lax.dot_general(q, k, _NT, preferred_element_type=_F32) * dec[h]
        st_h = state[h]
        o = _dot(s.astype(_BF16), v) + _dot(q, st_h.astype(_BF16)) * rowsc[h]
        kd = (k.astype(_F32) * colsc[h]).astype(_BF16)
        state[h] = (jnp.exp(lg_ref[h] * float(chunk)) * st_h
                    + lax.dot_general(kd, v, _TN, preferred_element_type=_F32))
        g = g_ref[:, h * RET_DV:(h + 1) * RET_DV]
        r_ref[:, h * RET_DV:(h + 1) * RET_DV] = (_rms(o) * _silu(g)).astype(r_ref.dtype)

    @pl.when(c == pl.num_programs(0) - 1)
    def _():
        st_ref[...] = state[...]


def _retention_prompt(lg, qkv, gate):
    s_len = qkv.shape[0]
    c = RET_CHUNK
    return pl.pallas_call(
        functools.partial(_ret_body, chunk=c),
        grid=(s_len // c,),
        in_specs=[pl.BlockSpec((N_HEADS, 1, 1), lambda i: (0, 0, 0)),
                  pl.BlockSpec((c, 1024), lambda i: (i, 0)),
                  pl.BlockSpec((c, 1024), lambda i: (i, 1)),
                  pl.BlockSpec((c, 2048), lambda i: (i, 1)),
                  pl.BlockSpec((c, 2048), lambda i: (i, 0))],
        out_specs=[pl.BlockSpec((c, 2048), lambda i: (i, 0)),
                   pl.BlockSpec((N_HEADS, RET_DK, RET_DV), lambda i: (0, 0, 0))],
        out_shape=[jax.ShapeDtypeStruct((s_len, 2048), _BF16),
                   jax.ShapeDtypeStruct((N_HEADS, RET_DK, RET_DV), _F32)],
        scratch_shapes=[pltpu.VMEM((N_HEADS, RET_DK, RET_DV), _F32),
                        pltpu.VMEM((N_HEADS, c, c), _F32),
                        pltpu.VMEM((N_HEADS, c, 1), _F32),
                        pltpu.VMEM((N_HEADS, c, 1), _F32)],
        compiler_params=_params("arbitrary"),
        name="retention_prompt",
    )(lg, qkv, qkv, qkv, gate)


def _ret_sample_body(lg_ref, q_ref, k_ref, v_ref, g_ref, st_ref, r_ref, nst_ref):
    eye = (lax.broadcasted_iota(jnp.int32, (RET_DK, RET_DK), 0)
           == lax.broadcasted_iota(jnp.int32, (RET_DK, RET_DK), 1))

    def column(row):
        return jnp.sum(jnp.where(eye, row, 0.0), axis=1, keepdims=True)

    for h in range(N_HEADS):
        qh = q_ref[0, h:h + 1, :]
        kh = k_ref[0, h:h + 1, :] * RET_KSCALE
        vh = v_ref[0, h:h + 1, :]
        st = st_ref[0, h]
        gam = jnp.exp(lg_ref[h])
        cross = jnp.sum(column(qh) * st, axis=0, keepdims=True) * gam
        qk = jnp.sum(qh * kh, axis=1, keepdims=True)
        o = qk * vh + cross
        nst_ref[0, h] = gam * st + column(kh) * vh
        r_ref[0, h:h + 1, :] = _rms(o) * _silu(g_ref[0, h:h + 1, :])


def _retention_sample(lg, q, k, v, gate, state):
    b = q.shape[0]
    vec = lambda d: pl.BlockSpec((1, N_HEADS, d), lambda i: (i, 0, 0))
    st_spec = pl.BlockSpec((1, N_HEADS, RET_DK, RET_DV), lambda i: (i, 0, 0, 0))
    return pl.pallas_call(
        _ret_sample_body,
        grid=(b,),
        in_specs=[pl.BlockSpec((N_HEADS, 1, 1), lambda i: (0, 0, 0)),
                  vec(RET_DK), vec(RET_DK), vec(RET_DV), vec(RET_DV), st_spec],
        out_specs=[vec(RET_DV), st_spec],
        out_shape=[jax.ShapeDtypeStruct((b, N_HEADS, RET_DV), _F32),
                   jax.ShapeDtypeStruct(state.shape, _F32)],
        compiler_params=_params("parallel"),
        name="retention_sample",
    )(lg, q, k, v, gate, state)


def _flash_body(qi_ref, kj_ref, slope_ref, q_ref, k_ref, v_ref, lam_ref, g_ref, o_ref,
                m_sc, l_sc, acc_sc, *, bq, bk):
    t = pl.program_id(1)
    qi = qi_ref[t]
    kj = kj_ref[t]

    @pl.when(kj == 0)
    def _():
        m_sc[...] = jnp.full(m_sc.shape, -1e30, _F32)
        l_sc[...] = jnp.zeros_like(l_sc)
        acc_sc[...] = jnp.zeros_like(acc_sc)

    col = lax.broadcasted_iota(jnp.int32, (1, bk), 1) + (kj * bk - qi * bq)
    bias = slope_ref[0] * col.astype(_F32)

    def update(masked):
        v = v_ref[...]
        for c in range(2):
            q = q_ref[:, c * DIFF_DH:(c + 1) * DIFF_DH]
            k = k_ref[:, c * DIFF_DH:(c + 1) * DIFF_DH]
            s = lax.dot_general(q, k, _NT, preferred_element_type=_F32) + bias
            if masked:
                row = lax.broadcasted_iota(jnp.int32, (bq, bk), 0) + qi * bq
                colp = lax.broadcasted_iota(jnp.int32, (bq, bk), 1) + kj * bk
                s = jnp.where(colp <= row, s, -jnp.inf)
            m_old = m_sc[c]
            m_new = jnp.maximum(m_old, jnp.max(s, axis=-1, keepdims=True))
            alpha = jnp.exp(m_old - m_new)
            p = jnp.exp(s - m_new)
            l_sc[c] = alpha * l_sc[c] + jnp.sum(p, axis=-1, keepdims=True)
            acc_sc[c] = alpha * acc_sc[c] + _dot(p.astype(_BF16), v)
            m_sc[c] = m_new

    crosses_diagonal = kj * bk + (bk - 1) > qi * bq
    pl.when(crosses_diagonal)(lambda: update(True))
    pl.when(jnp.logical_not(crosses_diagonal))(lambda: update(False))

    @pl.when(kj == ((qi + 1) * bq - 1) // bk)
    def _():
        o1 = acc_sc[0] / l_sc[0]
        o2 = acc_sc[1] / l_sc[1]
        o_ref[...] = _sub_ln(o1, o2, lam_ref, g_ref).astype(o_ref.dtype)


def _diff_prompt(slopes, q, k, v, lamv, subg):
    s_len = q.shape[0]
    bq, bk = FLASH_BQ, FLASH_BK
    pairs = [(i, j) for i in range(s_len // bq) for j in range(((i + 1) * bq - 1) // bk + 1)]
    qi_tab = jnp.asarray(np.array([p[0] for p in pairs], np.int32))
    kj_tab = jnp.asarray(np.array([p[1] for p in pairs], np.int32))
    grid_spec = pltpu.PrefetchScalarGridSpec(
        num_scalar_prefetch=2,
        grid=(N_HEADS, len(pairs)),
        in_specs=[pl.BlockSpec((1, 1, 1), lambda h, t, qi, kj: (h, 0, 0)),
                  pl.BlockSpec((bq, 256), lambda h, t, qi, kj: (qi[t], h)),
                  pl.BlockSpec((bk, 256), lambda h, t, qi, kj: (kj[t], h)),
                  pl.BlockSpec((bk, 256), lambda h, t, qi, kj: (kj[t], h)),
                  pl.BlockSpec((4, DIFF_DH), lambda h, t, qi, kj: (0, 0)),
                  pl.BlockSpec((1, DIFF_DV), lambda h, t, qi, kj: (0, 0))],
        out_specs=pl.BlockSpec((bq, 256), lambda h, t, qi, kj: (qi[t], h)),
        scratch_shapes=[pltpu.VMEM((2, bq, 1), _F32),
                        pltpu.VMEM((2, bq, 1), _F32),
                        pltpu.VMEM((2, bq, DIFF_DV), _F32)],
    )
    return pl.pallas_call(
        functools.partial(_flash_body, bq=bq, bk=bk),
        grid_spec=grid_spec,
        out_shape=jax.ShapeDtypeStruct((s_len, 2048), _BF16),
        compiler_params=_params("parallel", "arbitrary"),
        name="diff_attn_prompt",
    )(qi_tab, kj_tab, slopes, q, k, v, lamv, subg)


def _group_reduce(x, op, reduce_rows):
    r = x[:, 0:LANES]
    for t in range(1, x.shape[1] // LANES):
        r = op(r, x[:, t * LANES:(t + 1) * LANES])
    r = jnp.broadcast_to(reduce_rows(r, axis=0, keepdims=True), (SUBLANES, LANES))
    for shift in (K_ROWS, 2 * K_ROWS, 4 * K_ROWS):
        r = op(r, pltpu.roll(r, shift, axis=1))
    return r


def _pattern_to_column(pat):
    r = lax.broadcasted_iota(jnp.int32, (K_ROWS, LANES), 0)
    lane = lax.broadcasted_iota(jnp.int32, (K_ROWS, LANES), 1)
    own = lane == (r % N_HEADS) * 2 + r // N_HEADS
    pat16 = jnp.concatenate([pat, pat], axis=0)
    return jnp.sum(jnp.where(own, pat16, 0.0), axis=1, keepdims=True)


def _dec_body(pt_ref, q_ref, qc_ref, kc_ref, vc_ref, ba_ref, bb_ref, sel_ref, lam_ref, g_ref, *rest,
              pages, past_len):
    k_refs = rest[:pages]
    v_refs = rest[pages:2 * pages]
    o_ref = rest[2 * pages]
    m_sc, l_sc, acc_sc = rest[2 * pages + 1:]
    j = pl.program_id(1)
    page_rows = PAGE_SIZE * K_ROWS

    @pl.when(j == 0)
    def _():
        m_sc[...] = jnp.full(m_sc.shape, -1e30, _F32)
        l_sc[...] = jnp.zeros_like(l_sc)
        acc_sc[...] = jnp.zeros_like(acc_sc)

    own = (lax.broadcasted_iota(jnp.int32, (K_ROWS, page_rows), 1) % K_ROWS
           == lax.broadcasted_iota(jnp.int32, (K_ROWS, page_rows), 0))
    q = q_ref[0]
    rows = []
    for p in range(pages):
        g = lax.dot_general(q, k_refs[p][0], _NT, preferred_element_type=_F32)
        rows.append(jnp.sum(jnp.where(own, g, 0.0), axis=0, keepdims=True))
    s = jnp.concatenate(rows, axis=0)
    first_key = (past_len - j * (pages * PAGE_SIZE)).astype(_F32)
    s = s + (bb_ref[...] - ba_ref[...] * first_key)

    m_old = m_sc[...]
    m_new = jnp.maximum(m_old, _group_reduce(s, jnp.maximum, jnp.max))
    alpha = jnp.exp(m_old - m_new)
    p_all = jnp.exp(s - jnp.concatenate([m_new] * (page_rows // LANES), axis=1))
    l_sc[...] = alpha * l_sc[...] + _group_reduce(p_all, jnp.add, jnp.sum)
    m_sc[...] = m_new

    n_blk = page_rows // 256
    stacked = jnp.concatenate([p_all[:, b * 256:(b + 1) * 256] for b in range(n_blk)], axis=0)
    o = _dot(stacked.astype(_BF16), sel_ref[...])
    head_of_lane = (lax.broadcasted_iota(jnp.int32, (N_HEADS, LANES), 1) % N_HEADS
                    == lax.broadcasted_iota(jnp.int32, (N_HEADS, LANES), 0))
    acc = _pattern_to_column(alpha) * acc_sc[...]
    for p in range(pages):
        halves = []
        for c in range(2):
            blocks = []
            for b in range(n_blk):
                piece = o[b * pages + p:b * pages + p + 1, c * LANES:(c + 1) * LANES]
                blocks.append(jnp.where(head_of_lane, jnp.broadcast_to(piece, (N_HEADS, LANES)), 0.0))
            halves.append(jnp.concatenate(blocks, axis=1))
        lhs = jnp.concatenate(halves, axis=0).astype(_BF16)
        acc = acc + _dot(lhs, v_refs[p][0])
    acc_sc[...] = acc

    @pl.when(j == pl.num_programs(1) - 1)
    def _():
        m_col = _pattern_to_column(m_sc[...])
        l_col = _pattern_to_column(l_sc[...])
        s_new = jnp.sum(qc_ref[0] * kc_ref[0], axis=1, keepdims=True)
        m_fin = jnp.maximum(m_col, s_new)
        a = jnp.exp(m_col - m_fin)
        p_new = jnp.exp(s_new - m_fin)
        l_fin = a * l_col + p_new
        out = (a * acc_sc[...] + p_new * vc_ref[0]) / l_fin
        o_ref[0] = _sub_ln(out[0:N_HEADS], out[N_HEADS:], lam_ref, g_ref)


def _map_selector():
    sel = np.zeros((256, 256), np.float32)
    for key in range(16):
        for h in range(N_HEADS):
            for c in range(2):
                sel[key * K_ROWS + h * 2 + c, c * LANES + key * N_HEADS + h] = 1.0
    return jnp.asarray(sel, _BF16)


def _diff_sample(page_table, q, k_new, v_new, slopes, lamv, subg, cache_k, cache_v):
    b, n_pages = page_table.shape
    pages = DEC_PAGES_PER_STEP
    assert pages == SUBLANES and n_pages % pages == 0

    def map_major(a):
        return a.reshape(b, N_HEADS, 2, DIFF_DH).transpose(0, 2, 1, 3).reshape(b, K_ROWS, DIFF_DH)

    q16 = q.reshape(b, K_ROWS, DIFF_DH).astype(_BF16)
    v_rows = jnp.tile(v_new.reshape(b, N_HEADS, DIFF_DV), (1, 2, 1))
    lane_slope = jnp.tile(jnp.repeat(slopes, 2), PAGE_SIZE)
    key_of_lane = jnp.repeat(jnp.arange(PAGE_SIZE, dtype=_F32), K_ROWS)
    page_of_row = jnp.arange(pages, dtype=_F32)[:, None] * PAGE_SIZE
    bias_a = jnp.broadcast_to(lane_slope[None, :], (pages, PAGE_SIZE * K_ROWS))
    bias_b = lane_slope[None, :] * (page_of_row + key_of_lane[None, :])

    vec = lambda rows, d: pl.BlockSpec((1, rows, d), lambda i, j, pt: (i, 0, 0))
    const = lambda shape: pl.BlockSpec(shape, lambda i, j, pt: (0,) * len(shape))

    def page_spec(p, rows, d):
        return pl.BlockSpec((1, rows, d), lambda i, j, pt: (pt[i, j * pages + p], 0, 0))

    grid_spec = pltpu.PrefetchScalarGridSpec(
        num_scalar_prefetch=1,
        grid=(b, n_pages // pages),
        in_specs=([vec(K_ROWS, DIFF_DH), vec(K_ROWS, DIFF_DH), vec(K_ROWS, DIFF_DH), vec(K_ROWS, DIFF_DV),
                   const(bias_a.shape), const(bias_b.shape), const((256, 256)),
                   const((4, DIFF_DH)), const((1, DIFF_DV))]
                  + [page_spec(p, PAGE_SIZE * K_ROWS, DIFF_DH) for p in range(pages)]
                  + [page_spec(p, PAGE_SIZE * V_ROWS, DIFF_DV) for p in range(pages)]),
        out_specs=pl.BlockSpec((1, N_HEADS, DIFF_DV), lambda i, j, pt: (i, 0, 0)),
        scratch_shapes=[pltpu.VMEM((SUBLANES, LANES), _F32),
                        pltpu.VMEM((SUBLANES, LANES), _F32),
                        pltpu.VMEM((K_ROWS, DIFF_DV), _F32)],
    )
    return pl.pallas_call(
        functools.partial(_dec_body, pages=pages, past_len=n_pages * PAGE_SIZE),
        grid_spec=grid_spec,
        out_shape=jax.ShapeDtypeStruct((b, N_HEADS, DIFF_DV), _F32),
        compiler_params=_params("parallel", "arbitrary"),
        name="diff_attn_sample",
    )(page_table, q16, map_major(q), map_major(k_new), v_rows, bias_a, bias_b, _map_selector(), lamv, subg,
      *([cache_k] * pages), *([cache_v] * pages))


def _merge_body(r_ref, od_ref, wr_ref, wd_ref, ga_ref, gb_ref, o_ref):
    a = _dot(r_ref[...], wr_ref[...].astype(_BF16))
    b = _dot(od_ref[...], wd_ref[...].astype(_BF16))
    o_ref[...] = (jax.nn.sigmoid(ga_ref[...]) * a + jax.nn.sigmoid(gb_ref[...]) * b).astype(o_ref.dtype)


def _merge(r, od, w_ret_out, w_diff_out, gates, tm, tn=256):
    m = r.shape[0]
    nb = D_MODEL // tn
    return pl.pallas_call(
        _merge_body,
        grid=(m // tm, nb),
        in_specs=[pl.BlockSpec((tm, D_MODEL), lambda i, j: (i, 0)),
                  pl.BlockSpec((tm, D_MODEL), lambda i, j: (i, 0)),
                  pl.BlockSpec((D_MODEL, tn), lambda i, j: (0, j)),
                  pl.BlockSpec((D_MODEL, tn), lambda i, j: (0, j)),
                  pl.BlockSpec((tm, tn), lambda i, j: (i, j)),
                  pl.BlockSpec((tm, tn), lambda i, j: (i, nb + j))],
        out_specs=pl.BlockSpec((tm, tn), lambda i, j: (i, j)),
        out_shape=jax.ShapeDtypeStruct((m, D_MODEL), _BF16),
        compiler_params=_params("parallel", "arbitrary"),
        name="merge_gate",
    )(r, od, w_ret_out, w_diff_out, gates, gates)


def _outproj_body(a_ref, w_ref, x_ref, g_ref, b_ref, o_ref, acc_sc):
    k = pl.program_id(1)

    @pl.when(k == 0)
    def _():
        acc_sc[...] = jnp.zeros_like(acc_sc)

    acc_sc[...] += _dot(a_ref[...], w_ref[...].astype(_BF16))

    @pl.when(k == pl.num_programs(1) - 1)
    def _():
        o_ref[...] = _layer_norm(DN_ALPHA * x_ref[...] + acc_sc[...], g_ref[...], b_ref[...])


def _outproj_ln(a, w_out, x, g, b, tm, tk=512):
    m = a.shape[0]
    return pl.pallas_call(
        _outproj_body,
        grid=(m // tm, D_MODEL // tk),
        in_specs=[pl.BlockSpec((tm, tk), lambda i, k: (i, k)),
                  pl.BlockSpec((tk, D_MODEL), lambda i, k: (k, 0)),
                  pl.BlockSpec((tm, D_MODEL), lambda i, k: (i, 0)),
                  pl.BlockSpec((1, D_MODEL), lambda i, k: (0, 0)),
                  pl.BlockSpec((1, D_MODEL), lambda i, k: (0, 0))],
        out_specs=pl.BlockSpec((tm, D_MODEL), lambda i, k: (i, 0)),
        out_shape=jax.ShapeDtypeStruct((m, D_MODEL), _F32),
        scratch_shapes=[pltpu.VMEM((tm, D_MODEL), _F32)],
        compiler_params=_params("parallel", "arbitrary"),
        name="outproj_ln",
    )(a, w_out, x, g, b)


def _mlp_body(x_ref, wu_ref, wd_ref, g_ref, b_ref, o_ref, xb_sc, acc_sc):
    f = pl.program_id(1)

    @pl.when(f == 0)
    def _():
        xb_sc[...] = x_ref[...].astype(_BF16)
        acc_sc[...] = jnp.zeros_like(acc_sc)

    hid = jnp.maximum(_dot(xb_sc[...], wu_ref[...]), 0.0)
    acc_sc[...] += _dot((hid * hid).astype(_BF16), wd_ref[...])

    @pl.when(f == pl.num_programs(1) - 1)
    def _():
        o_ref[...] = _layer_norm(DN_ALPHA * x_ref[...] + acc_sc[...], g_ref[...], b_ref[...])


def _mlp_ln(x, w_up, w_down, g, b, tm, tf=512):
    m = x.shape[0]
    return pl.pallas_call(
        _mlp_body,
        grid=(m // tm, D_FF // tf),
        in_specs=[pl.BlockSpec((tm, D_MODEL), lambda i, f: (i, 0)),
                  pl.BlockSpec((D_MODEL, tf), lambda i, f: (0, f)),
                  pl.BlockSpec((tf, D_MODEL), lambda i, f: (f, 0)),
                  pl.BlockSpec((1, D_MODEL), lambda i, f: (0, 0)),
                  pl.BlockSpec((1, D_MODEL), lambda i, f: (0, 0))],
        out_specs=pl.BlockSpec((tm, D_MODEL), lambda i, f: (i, 0)),
        out_shape=jax.ShapeDtypeStruct((m, D_MODEL), _F32),
        scratch_shapes=[pltpu.VMEM((tm, D_MODEL), _BF16),
                        pltpu.VMEM((tm, D_MODEL), _F32)],
        compiler_params=_params("parallel", "arbitrary"),
        name="mlp_ln",
    )(x, w_up, w_down, g, b)


def _finish(x, r, od, gates, w, tm_merge, tm):
    merged = _merge(r, od, w["ret_out"], w["diff_out"], gates, tm_merge)
    x1 = _outproj_ln(merged, w["out"], x, w["ln1_g"], w["ln1_b"], tm)
    return _mlp_ln(x1, w["up"], w["down"], w["ln2_g"], w["ln2_b"], tm)


def kernel(x_prompt, x_sample, cache_k, cache_v, state_ret, page_table, w_in, w_ret_out, w_diff_out, w_out, lambda_q1, lambda_k1, lambda_q2, lambda_k2, subln_g, ln1_g, ln1_b, ln2_g, ln2_b, w_up, w_down):
    assert w_in.shape[0] == 1, "single-layer step"
    seq = x_prompt.shape[1]
    dec_b = x_sample.shape[0]
    n_pool = cache_k.shape[1]
    w_in0 = w_in[0]
    w = {"ret_out": w_ret_out[0], "diff_out": w_diff_out[0], "out": w_out[0],
         "ln1_g": ln1_g, "ln1_b": ln1_b, "ln2_g": ln2_g, "ln2_b": ln2_b,
         "up": w_up[0].astype(_BF16), "down": w_down[0].astype(_BF16)}
    heads = jnp.arange(N_HEADS, dtype=_F32)
    log_gamma = jnp.log1p(-jnp.exp2(-5.0 - heads)).reshape(N_HEADS, 1, 1)
    slopes = jnp.exp2(-8.0 * (heads + 1.0) / N_HEADS)
    lamv = jnp.concatenate([lambda_q1, lambda_k1, lambda_q2, lambda_k2], axis=0)
    subg = subln_g.reshape(1, DIFF_DV)

    xp = x_prompt[0]
    xp16 = xp.astype(_BF16)
    (ret_qkv,) = _proj(xp16, w_in0, COL_RET_QKV, 4096, [_BF16])
    (ret_gate,) = _proj(xp16, w_in0, COL_RET_GATE, 2048, [_F32])
    (dq,) = _proj(xp16, w_in0, COL_DIFF_Q, 2048, [_BF16], scale=DIFF_QSCALE)
    k_rows, dk16 = _proj_k(xp16, w_in0, COL_DIFF_K)
    dv, dv16 = _proj(xp16, w_in0, COL_DIFF_V, 2048, [_F32, _BF16])
    (gates,) = _proj(xp16, w_in0, COL_MERGE_GATES, 4096, [_F32])
    r_p, st_p = _retention_prompt(log_gamma, ret_qkv, ret_gate)
    od_p = _diff_prompt(slopes.reshape(N_HEADS, 1, 1), dq, dk16, dv16, lamv, subg)
    y_p = _finish(xp, r_p, od_p, gates, w, tm_merge=1024, tm=512)

    xs = x_sample[:, 0]
    xs16 = xs.astype(_BF16)
    (ret_s,) = _proj(xs16, w_in0, COL_RET_QKV, 6144, [_F32])
    (dq_s,) = _proj(xs16, w_in0, COL_DIFF_Q, 2048, [_F32], scale=DIFF_QSCALE)
    (rest_s,) = _proj(xs16, w_in0, COL_DIFF_K, 8192, [_F32])
    dk_s = rest_s[:, 0:2048]
    dv_s = rest_s[:, 2048:4096]
    gates_s = rest_s[:, 4096:8192]
    r_s, st_s = _retention_sample(
        log_gamma,
        ret_s[:, 0:1024].reshape(dec_b, N_HEADS, RET_DK),
        ret_s[:, 1024:2048].reshape(dec_b, N_HEADS, RET_DK),
        ret_s[:, 2048:4096].reshape(dec_b, N_HEADS, RET_DV),
        ret_s[:, 4096:6144].reshape(dec_b, N_HEADS, RET_DV),
        state_ret[0])
    od_s = _diff_sample(
        page_table, dq_s, dk_s, dv_s, slopes, lamv, subg,
        cache_k.reshape(n_pool, PAGE_SIZE * K_ROWS, DIFF_DH),
        cache_v.reshape(n_pool, PAGE_SIZE * V_ROWS, DIFF_DV))
    y_s = _finish(xs, r_s.reshape(dec_b, D_MODEL).astype(_BF16), od_s.reshape(dec_b, D_MODEL).astype(_BF16),
                  gates_s, w, tm_merge=dec_b, tm=dec_b)

    return (y_p[None], y_s[:, None, :],
            k_rows.reshape(1, 1, seq, N_HEADS, 2, DIFF_DH), dv.reshape(1, 1, seq, N_HEADS, DIFF_DV),
            st_p[None, None],
            dk_s.reshape(1, dec_b, 1, N_HEADS, 2, DIFF_DH), dv_s.reshape(1, dec_b, 1, N_HEADS, DIFF_DV),
            st_s[None])
```

```python
import functools
import math

import jax
import jax.numpy as jnp
import numpy as np
from jax import lax
from jax.experimental import pallas as pl
from jax.experimental.pallas import tpu as pltpu

D_MODEL = 2048
N_HEADS = 8
RET_DK = 128
RET_DV = 256
DIFF_DH = 128
DIFF_DV = 256
D_FF = 4 * D_MODEL
PAGE_SIZE = 128
DN_ALPHA = 2.0 ** 0.25
LAM_INIT = 0.8 - 0.6 * math.exp(-0.3 * 0)
LN_EPS = 1e-5
NORM_EPS = 1e-5
RET_KSCALE = RET_DK ** -0.5
DIFF_QSCALE = DIFF_DH ** -0.5

COL_RET_QKV = 0
COL_RET_GATE = 4096
COL_DIFF_Q = 6144
COL_DIFF_K = 8192
COL_DIFF_V = 10240
COL_MERGE_GATES = 12288

V7X_VMEM_LIMIT_BYTES = 56 * 1024 * 1024
LANES = 128
SUBLANES = 8

RET_CHUNK = 256
FLASH_BQ = 1024
FLASH_BK = 1024
FLASH_ROW_CHUNK = 32
LOG2_E = math.log2(math.e)
DEC_PAGES_PER_STEP = 8

K_ROWS = 2 * N_HEADS
V_ROWS = N_HEADS

_NT = (((1,), (1,)), ((), ()))
_TN = (((0,), (0,)), ((), ()))
_BF16 = jnp.bfloat16
_F32 = jnp.float32


def _params(*semantics):
    return pltpu.CompilerParams(dimension_semantics=semantics,
                                vmem_limit_bytes=V7X_VMEM_LIMIT_BYTES)


def _dot(a, b):
    return jnp.dot(a, b, preferred_element_type=_F32)


def _layer_norm(y, g, b):
    mu = jnp.mean(y, -1, keepdims=True)
    yc = y - mu
    var = jnp.mean(yc * yc, -1, keepdims=True)
    return yc * lax.rsqrt(var + LN_EPS) * g + b


def _rms(y):
    return y * lax.rsqrt(jnp.mean(y * y, -1, keepdims=True) + NORM_EPS)


def _silu(g):
    return g * jax.nn.sigmoid(g)


def _diff_lambda(lam_ref):
    v = lam_ref[...]
    a = jnp.sum(v[0:1] * v[1:2], axis=-1, keepdims=True)
    b = jnp.sum(v[2:3] * v[3:4], axis=-1, keepdims=True)
    return jnp.exp(a) - jnp.exp(b) + LAM_INIT


def _sub_ln(o1, o2, lam_ref, g_ref):
    od = o1 - _diff_lambda(lam_ref) * o2
    return _rms(od) * g_ref[...] * (1.0 - LAM_INIT)


def _proj_body(x_ref, w_ref, *o_refs, scale):
    acc = _dot(x_ref[...], w_ref[...].astype(_BF16))
    if scale is not None:
        acc = acc * scale
    for o_ref in o_refs:
        o_ref[...] = acc.astype(o_ref.dtype)


def _proj(x, w, col0, ncols, out_dtypes, scale=None, tn=512):
    m, kd = x.shape
    tm = min(2048, m)
    c0 = col0 // tn
    return pl.pallas_call(
        functools.partial(_proj_body, scale=scale),
        grid=(m // tm, ncols // tn),
        in_specs=[pl.BlockSpec((tm, kd), lambda i, j: (i, 0)),
                  pl.BlockSpec((kd, tn), lambda i, j: (0, c0 + j))],
        out_specs=[pl.BlockSpec((tm, tn), lambda i, j: (i, j)) for _ in out_dtypes],
        out_shape=[jax.ShapeDtypeStruct((m, ncols), dt) for dt in out_dtypes],
        compiler_params=_params("parallel", "arbitrary"),
        name="proj",
    )(x, w)


def _proj_k_body(x_ref, w_ref, cache_ref, o16_ref, *, tm):
    h = pl.program_id(1)
    acc = _dot(x_ref[...], w_ref[...].astype(_BF16))
    o16_ref[...] = acc.astype(_BF16)
    for c in range(2):
        cache_ref[pl.ds(h * 2 + c, tm, stride=K_ROWS), :] = acc[:, c * DIFF_DH:(c + 1) * DIFF_DH]


def _proj_k(x, w, col0, tm=1024):
    m, kd = x.shape
    c0 = col0 // 256
    return pl.pallas_call(
        functools.partial(_proj_k_body, tm=tm),
        grid=(m // tm, N_HEADS),
        in_specs=[pl.BlockSpec((tm, kd), lambda i, h: (i, 0)),
                  pl.BlockSpec((kd, 256), lambda i, h: (0, c0 + h))],
        out_specs=[pl.BlockSpec((tm * K_ROWS, DIFF_DH), lambda i, h: (i, 0)),
                   pl.BlockSpec((tm, 256), lambda i, h: (i, h))],
        out_shape=[jax.ShapeDtypeStruct((m * K_ROWS, DIFF_DH), _F32),
                   jax.ShapeDtypeStruct((m, D_MODEL), _BF16)],
        compiler_params=_params("parallel", "arbitrary"),
        name="proj_k",
    )(x, w)


def _ret_body(lg_ref, q_ref, k_ref, v_ref, g_ref, r_ref, st_ref, state, dec, rowsc, colsc, *, chunk):
    c = pl.program_id(0)

    @pl.when(c == 0)
    def _():
        state[...] = jnp.zeros_like(state)
        i = lax.broadcasted_iota(jnp.int32, (chunk, chunk), 0)
        j = lax.broadcasted_iota(jnp.int32, (chunk, chunk), 1)
        rel = (i - j).astype(_F32)
        pos = lax.broadcasted_iota(jnp.int32, (chunk, 1), 0).astype(_F32)
        for h in range(N_HEADS):
            lg = lg_ref[h]
            dec[h] = jnp.where(rel >= 0, jnp.exp(lg * jnp.maximum(rel, 0.0)), 0.0) * RET_KSCALE
            rowsc[h] = jnp.exp(lg * (pos + 1.0))
            colsc[h] = jnp.exp(lg * (chunk - 1.0 - pos)) * RET_KSCALE

    for h in range(N_HEADS):
        q = q_ref[:, h * RET_DK:(h + 1) * RET_DK]
        k = k_ref[:, h * RET_DK:(h + 1) * RET_DK]
        v = v_ref[:, h * RET_DV:(h + 1) * RET_DV]
        s = lax.dot_general(q, k, _NT, preferred_element_type=_F32) * dec[h]
        st_h = state[h]
        o = _dot(s.astype(_BF16), v) + _dot(q, st_h.astype(_BF16)) * rowsc[h]
        kd = (k.astype(_F32) * colsc[h]).astype(_BF16)
        state[h] = (jnp.exp(lg_ref[h] * float(chunk)) * st_h
                    + lax.dot_general(kd, v, _TN, preferred_element_type=_F32))
        g = g_ref[:, h * RET_DV:(h + 1) * RET_DV]
        r_ref[:, h * RET_DV:(h + 1) * RET_DV] = (_rms(o) * _silu(g)).astype(r_ref.dtype)

    @pl.when(c == pl.num_programs(0) - 1)
    def _():
        st_ref[...] = state[...]


def _retention_prompt(lg, qkv, gate):
    s_len = qkv.shape[0]
    c = RET_CHUNK
    return pl.pallas_call(
        functools.partial(_ret_body, chunk=c),
        grid=(s_len // c,),
        in_specs=[pl.BlockSpec((N_HEADS, 1, 1), lambda i: (0, 0, 0)),
                  pl.BlockSpec((c, 1024), lambda i: (i, 0)),
                  pl.BlockSpec((c, 1024), lambda i: (i, 1)),
                  pl.BlockSpec((c, 2048), lambda i: (i, 1)),
                  pl.BlockSpec((c, 2048), lambda i: (i, 0))],
        out_specs=[pl.BlockSpec((c, 2048), lambda i: (i, 0)),
                   pl.BlockSpec((N_HEADS, RET_DK, RET_DV), lambda i: (0, 0, 0))],
        out_shape=[jax.ShapeDtypeStruct((s_len, 2048), _BF16),
                   jax.ShapeDtypeStruct((N_HEADS, RET_DK, RET_DV), _F32)],
        scratch_shapes=[pltpu.VMEM((N_HEADS, RET_DK, RET_DV), _F32),
                        pltpu.VMEM((N_HEADS, c, c), _F32),
                        pltpu.VMEM((N_HEADS, c, 1), _F32),
                        pltpu.VMEM((N_HEADS, c, 1), _F32)],
        compiler_params=_params("arbitrary"),
        name="retention_prompt",
    )(lg, qkv, qkv, qkv, gate)


def _ret_sample_body(lg_ref, q_ref, k_ref, v_ref, g_ref, st_ref, r_ref, nst_ref):
    eye = (lax.broadcasted_iota(jnp.int32, (RET_DK, RET_DK), 0)
           == lax.broadcasted_iota(jnp.int32, (RET_DK, RET_DK), 1))

    def column(row):
        return jnp.sum(jnp.where(eye, row, 0.0), axis=1, keepdims=True)

    for h in range(N_HEADS):
        qh = q_ref[0, h:h + 1, :]
        kh = k_ref[0, h:h + 1, :] * RET_KSCALE
        vh = v_ref[0, h:h + 1, :]
        st = st_ref[0, h]
        gam = jnp.exp(lg_ref[h])
        cross = jnp.sum(column(qh) * st, axis=0, keepdims=True) * gam
        qk = jnp.sum(qh * kh, axis=1, keepdims=True)
        o = qk * vh + cross
        nst_ref[0, h] = gam * st + column(kh) * vh
        r_ref[0, h:h + 1, :] = _rms(o) * _silu(g_ref[0, h:h + 1, :])


def _retention_sample(lg, q, k, v, gate, state):
    b = q.shape[0]
    vec = lambda d: pl.BlockSpec((1, N_HEADS, d), lambda i: (i, 0, 0))
    st_spec = pl.BlockSpec((1, N_HEADS, RET_DK, RET_DV), lambda i: (i, 0, 0, 0))
    return pl.pallas_call(
        _ret_sample_body,
        grid=(b,),
        in_specs=[pl.BlockSpec((N_HEADS, 1, 1), lambda i: (0, 0, 0)),
                  vec(RET_DK), vec(RET_DK), vec(RET_DV), vec(RET_DV), st_spec],
        out_specs=[vec(RET_DV), st_spec],
        out_shape=[jax.ShapeDtypeStruct((b, N_HEADS, RET_DV), _F32),
                   jax.ShapeDtypeStruct(state.shape, _F32)],
        compiler_params=_params("parallel"),
        name="retention_sample",
    )(lg, q, k, v, gate, state)


def _flash_body(qi_ref, kj_ref, slope_ref, q_ref, k_ref, v_ref, lam_ref, g_ref, o_ref,
                m_sc, l_sc, acc_sc, s_sc, p_sc, a_sc, *, bq, bk, rc):
    t = pl.program_id(1)
    qi = qi_ref[t]
    kj = kj_ref[t]

    @pl.when(kj == 0)
    def _():
        m_sc[...] = jnp.full(m_sc.shape, -1e30, _F32)
        l_sc[...] = jnp.zeros_like(l_sc)
        acc_sc[...] = jnp.zeros_like(acc_sc)

    col = lax.broadcasted_iota(jnp.int32, (1, bk), 1) + (kj * bk - qi * bq)
    bias = slope_ref[0] * col.astype(_F32)
    n_rep = bk // LANES

    def update(masked):
        v = v_ref[...]
        for c in range(2):
            q = q_ref[:, c * DIFF_DH:(c + 1) * DIFF_DH]
            k = k_ref[:, c * DIFF_DH:(c + 1) * DIFF_DH]
            s_sc[c] = lax.dot_general(q, k, _NT, preferred_element_type=_F32)
        for c in range(2):
            for r in range(bq // rc):
                rows = slice(r * rc, (r + 1) * rc)
                s = s_sc[c, rows, :] + bias
                if masked:
                    rowi = lax.broadcasted_iota(jnp.int32, (rc, bk), 0) + (qi * bq + r * rc)
                    colp = lax.broadcasted_iota(jnp.int32, (rc, bk), 1) + kj * bk
                    s = jnp.where(colp <= rowi, s, -jnp.inf)
                m_old = m_sc[c, rows, :]
                m_new = jnp.maximum(m_old, jnp.max(s, axis=-1, keepdims=True))
                alpha = jnp.exp2(m_old - m_new)
                p = jnp.exp2(s - jnp.concatenate([m_new] * n_rep, axis=1))
                l_sc[c, rows, :] = alpha * l_sc[c, rows, :] + jnp.sum(p, axis=-1, keepdims=True)
                m_sc[c, rows, :] = m_new
                a_sc[c, rows, :] = alpha
                p_sc[c, rows, :] = p.astype(_BF16)
            a = a_sc[c]
            acc_sc[c] = jnp.concatenate([a, a], axis=1) * acc_sc[c] + _dot(p_sc[c], v)

    crosses_diagonal = kj * bk + (bk - 1) > qi * bq
    pl.when(crosses_diagonal)(lambda: update(True))
    pl.when(jnp.logical_not(crosses_diagonal))(lambda: update(False))

    @pl.when(kj == ((qi + 1) * bq - 1) // bk)
    def _():
        l0 = l_sc[0]
        l1 = l_sc[1]
        o1 = acc_sc[0] / jnp.concatenate([l0, l0], axis=1)
        o2 = acc_sc[1] / jnp.concatenate([l1, l1], axis=1)
        o_ref[...] = _sub_ln(o1, o2, lam_ref, g_ref).astype(o_ref.dtype)


def _diff_prompt(slopes, q, k, v, lamv, subg):
    s_len = q.shape[0]
    bq, bk, rc = FLASH_BQ, FLASH_BK, FLASH_ROW_CHUNK
    pairs = [(i, j) for i in range(s_len // bq) for j in range(((i + 1) * bq - 1) // bk + 1)]
    qi_tab = jnp.asarray(np.array([p[0] for p in pairs], np.int32))
    kj_tab = jnp.asarray(np.array([p[1] for p in pairs], np.int32))
    grid_spec = pltpu.PrefetchScalarGridSpec(
        num_scalar_prefetch=2,
        grid=(N_HEADS, len(pairs)),
        in_specs=[pl.BlockSpec((1, 1, 1), lambda h, t, qi, kj: (h, 0, 0)),
                  pl.BlockSpec((bq, 256), lambda h, t, qi, kj: (qi[t], h)),
                  pl.BlockSpec((bk, 256), lambda h, t, qi, kj: (kj[t], h)),
                  pl.BlockSpec((bk, 256), lambda h, t, qi, kj: (kj[t], h)),
                  pl.BlockSpec((4, DIFF_DH), lambda h, t, qi, kj: (0, 0)),
                  pl.BlockSpec((1, DIFF_DV), lambda h, t, qi, kj: (0, 0))],
        out_specs=pl.BlockSpec((bq, 256), lambda h, t, qi, kj: (qi[t], h)),
        scratch_shapes=[pltpu.VMEM((2, bq, LANES), _F32),
                        pltpu.VMEM((2, bq, LANES), _F32),
                        pltpu.VMEM((2, bq, DIFF_DV), _F32),
                        pltpu.VMEM((2, bq, bk), _F32),
                        pltpu.VMEM((2, bq, bk), _BF16),
                        pltpu.VMEM((2, bq, LANES), _F32)],
    )
    return pl.pallas_call(
        functools.partial(_flash_body, bq=bq, bk=bk, rc=rc),
        grid_spec=grid_spec,
        out_shape=jax.ShapeDtypeStruct((s_len, 2048), _BF16),
        compiler_params=_params("parallel", "arbitrary"),
        name="diff_attn_prompt",
    )(qi_tab, kj_tab, slopes, q, k, v, lamv, subg)


def _group_reduce(x, op, reduce_rows):
    r = x[:, 0:LANES]
    for t in range(1, x.shape[1] // LANES):
        r = op(r, x[:, t * LANES:(t + 1) * LANES])
    r = jnp.broadcast_to(reduce_rows(r, axis=0, keepdims=True), (SUBLANES, LANES))
    for shift in (K_ROWS, 2 * K_ROWS, 4 * K_ROWS):
        r = op(r, pltpu.roll(r, shift, axis=1))
    return r


def _pattern_to_column(pat):
    r = lax.broadcasted_iota(jnp.int32, (K_ROWS, LANES), 0)
    lane = lax.broadcasted_iota(jnp.int32, (K_ROWS, LANES), 1)
    own = lane == (r % N_HEADS) * 2 + r // N_HEADS
    pat16 = jnp.concatenate([pat, pat], axis=0)
    return jnp.sum(jnp.where(own, pat16, 0.0), axis=1, keepdims=True)


def _dec_body(pt_ref, q_ref, qc_ref, kc_ref, vc_ref, ba_ref, bb_ref, sel_ref, lam_ref, g_ref, *rest,
              pages, past_len):
    k_refs = rest[:pages]
    v_refs = rest[pages:2 * pages]
    o_ref = rest[2 * pages]
    m_sc, l_sc, acc_sc = rest[2 * pages + 1:]
    j = pl.program_id(1)
    page_rows = PAGE_SIZE * K_ROWS

    @pl.when(j == 0)
    def _():
        m_sc[...] = jnp.full(m_sc.shape, -1e30, _F32)
        l_sc[...] = jnp.zeros_like(l_sc)
        acc_sc[...] = jnp.zeros_like(acc_sc)

    own = (lax.broadcasted_iota(jnp.int32, (K_ROWS, page_rows), 1) % K_ROWS
           == lax.broadcasted_iota(jnp.int32, (K_ROWS, page_rows), 0))
    q = q_ref[0]
    rows = []
    for p in range(pages):
        g = lax.dot_general(q, k_refs[p][0], _NT, preferred_element_type=_F32)
        rows.append(jnp.sum(jnp.where(own, g, 0.0), axis=0, keepdims=True))
    s = jnp.concatenate(rows, axis=0)
    first_key = (past_len - j * (pages * PAGE_SIZE)).astype(_F32)
    s = s + (bb_ref[...] - ba_ref[...] * first_key)

    m_old = m_sc[...]
    m_new = jnp.maximum(m_old, _group_reduce(s, jnp.maximum, jnp.max))
    alpha = jnp.exp(m_old - m_new)
    p_all = jnp.exp(s - jnp.concatenate([m_new] * (page_rows // LANES), axis=1))
    l_sc[...] = alpha * l_sc[...] + _group_reduce(p_all, jnp.add, jnp.sum)
    m_sc[...] = m_new

    n_blk = page_rows // 256
    stacked = jnp.concatenate([p_all[:, b * 256:(b + 1) * 256] for b in range(n_blk)], axis=0)
    o = _dot(stacked.astype(_BF16), sel_ref[...])
    head_of_lane = (lax.broadcasted_iota(jnp.int32, (N_HEADS, LANES), 1) % N_HEADS
                    == lax.broadcasted_iota(jnp.int32, (N_HEADS, LANES), 0))
    acc = _pattern_to_column(alpha) * acc_sc[...]
    for p in range(pages):
        halves = []
        for c in range(2):
            blocks = []
            for b in range(n_blk):
                piece = o[b * pages + p:b * pages + p + 1, c * LANES:(c + 1) * LANES]
                blocks.append(jnp.where(head_of_lane, jnp.broadcast_to(piece, (N_HEADS, LANES)), 0.0))
            halves.append(jnp.concatenate(blocks, axis=1))
        lhs = jnp.concatenate(halves, axis=0).astype(_BF16)
        acc = acc + _dot(lhs, v_refs[p][0])
    acc_sc[...] = acc

    @pl.when(j == pl.num_programs(1) - 1)
    def _():
        m_col = _pattern_to_column(m_sc[...])
        l_col = _pattern_to_column(l_sc[...])
        s_new = jnp.sum(qc_ref[0] * kc_ref[0], axis=1, keepdims=True)
        m_fin = jnp.maximum(m_col, s_new)
        a = jnp.exp(m_col - m_fin)
        p_new = jnp.exp(s_new - m_fin)
        l_fin = a * l_col + p_new
        out = (a * acc_sc[...] + p_new * vc_ref[0]) / l_fin
        o_ref[0] = _sub_ln(out[0:N_HEADS], out[N_HEADS:], lam_ref, g_ref)


def _map_selector():
    sel = np.zeros((256, 256), np.float32)
    for key in range(16):
        for h in range(N_HEADS):
            for c in range(2):
                sel[key * K_ROWS + h * 2 + c, c * LANES + key * N_HEADS + h] = 1.0
    return jnp.asarray(sel, _BF16)


def _diff_sample(page_table, q, k_new, v_new, slopes, lamv, subg, cache_k, cache_v):
    b, n_pages = page_table.shape
    pages = DEC_PAGES_PER_STEP
    assert pages == SUBLANES and n_pages % pages == 0

    def map_major(a):
        return a.reshape(b, N_HEADS, 2, DIFF_DH).transpose(0, 2, 1, 3).reshape(b, K_ROWS, DIFF_DH)

    q16 = q.reshape(b, K_ROWS, DIFF_DH).astype(_BF16)
    v_rows = jnp.tile(v_new.reshape(b, N_HEADS, DIFF_DV), (1, 2, 1))
    lane_slope = jnp.tile(jnp.repeat(slopes, 2), PAGE_SIZE)
    key_of_lane = jnp.repeat(jnp.arange(PAGE_SIZE, dtype=_F32), K_ROWS)
    page_of_row = jnp.arange(pages, dtype=_F32)[:, None] * PAGE_SIZE
    bias_a = jnp.broadcast_to(lane_slope[None, :], (pages, PAGE_SIZE * K_ROWS))
    bias_b = lane_slope[None, :] * (page_of_row + key_of_lane[None, :])

    vec = lambda rows, d: pl.BlockSpec((1, rows, d), lambda i, j, pt: (i, 0, 0))
    const = lambda shape: pl.BlockSpec(shape, lambda i, j, pt: (0,) * len(shape))

    def page_spec(p, rows, d):
        return pl.BlockSpec((1, rows, d), lambda i, j, pt: (pt[i, j * pages + p], 0, 0))

    grid_spec = pltpu.PrefetchScalarGridSpec(
        num_scalar_prefetch=1,
        grid=(b, n_pages // pages),
        in_specs=([vec(K_ROWS, DIFF_DH), vec(K_ROWS, DIFF_DH), vec(K_ROWS, DIFF_DH), vec(K_ROWS, DIFF_DV),
                   const(bias_a.shape), const(bias_b.shape), const((256, 256)),
                   const((4, DIFF_DH)), const((1, DIFF_DV))]
                  + [page_spec(p, PAGE_SIZE * K_ROWS, DIFF_DH) for p in range(pages)]
                  + [page_spec(p, PAGE_SIZE * V_ROWS, DIFF_DV) for p in range(pages)]),
        out_specs=pl.BlockSpec((1, N_HEADS, DIFF_DV), lambda i, j, pt: (i, 0, 0)),
        scratch_shapes=[pltpu.VMEM((SUBLANES, LANES), _F32),
                        pltpu.VMEM((SUBLANES, LANES), _F32),
                        pltpu.VMEM((K_ROWS, DIFF_DV), _F32)],
    )
    return pl.pallas_call(
        functools.partial(_dec_body, pages=pages, past_len=n_pages * PAGE_SIZE),
        grid_spec=grid_spec,
        out_shape=jax.ShapeDtypeStruct((b, N_HEADS, DIFF_DV), _F32),
        compiler_params=_params("parallel", "arbitrary"),
        name="diff_attn_sample",
    )(page_table, q16, map_major(q), map_major(k_new), v_rows, bias_a, bias_b, _map_selector(), lamv, subg,
      *([cache_k] * pages), *([cache_v] * pages))


def _merge_body(r_ref, od_ref, wr_ref, wd_ref, ga_ref, gb_ref, o_ref):
    a = _dot(r_ref[...], wr_ref[...])
    b = _dot(od_ref[...], wd_ref[...])
    o_ref[...] = (jax.nn.sigmoid(ga_ref[...]) * a + jax.nn.sigmoid(gb_ref[...]) * b).astype(o_ref.dtype)


def _merge(r, od, w_ret_out, w_diff_out, gates, tm, tn=256):
    m = r.shape[0]
    nb = D_MODEL // tn
    return pl.pallas_call(
        _merge_body,
        grid=(m // tm, nb),
        in_specs=[pl.BlockSpec((tm, D_MODEL), lambda i, j: (i, 0)),
                  pl.BlockSpec((tm, D_MODEL), lambda i, j: (i, 0)),
                  pl.BlockSpec((D_MODEL, tn), lambda i, j: (0, j)),
                  pl.BlockSpec((D_MODEL, tn), lambda i, j: (0, j)),
                  pl.BlockSpec((tm, tn), lambda i, j: (i, j)),
                  pl.BlockSpec((tm, tn), lambda i, j: (i, nb + j))],
        out_specs=pl.BlockSpec((tm, tn), lambda i, j: (i, j)),
        out_shape=jax.ShapeDtypeStruct((m, D_MODEL), _BF16),
        compiler_params=_params("parallel", "arbitrary"),
        name="merge_gate",
    )(r, od, w_ret_out, w_diff_out, gates, gates)


def _outproj_body(a_ref, w_ref, x_ref, g_ref, b_ref, o_ref, acc_sc):
    k = pl.program_id(1)

    @pl.when(k == 0)
    def _():
        acc_sc[...] = jnp.zeros_like(acc_sc)

    acc_sc[...] += _dot(a_ref[...], w_ref[...])

    @pl.when(k == pl.num_programs(1) - 1)
    def _():
        o_ref[...] = _layer_norm(DN_ALPHA * x_ref[...] + acc_sc[...], g_ref[...], b_ref[...])


def _outproj_ln(a, w_out, x, g, b, tm, tk=512):
    m = a.shape[0]
    return pl.pallas_call(
        _outproj_body,
        grid=(m // tm, D_MODEL // tk),
        in_specs=[pl.BlockSpec((tm, tk), lambda i, k: (i, k)),
                  pl.BlockSpec((tk, D_MODEL), lambda i, k: (k, 0)),
                  pl.BlockSpec((tm, D_MODEL), lambda i, k: (i, 0)),
                  pl.BlockSpec((1, D_MODEL), lambda i, k: (0, 0)),
                  pl.BlockSpec((1, D_MODEL), lambda i, k: (0, 0))],
        out_specs=pl.BlockSpec((tm, D_MODEL), lambda i, k: (i, 0)),
        out_shape=jax.ShapeDtypeStruct((m, D_MODEL), _F32),
        scratch_shapes=[pltpu.VMEM((tm, D_MODEL), _F32)],
        compiler_params=_params("parallel", "arbitrary"),
        name="outproj_ln",
    )(a, w_out, x, g, b)


def _mlp_body(x_ref, wu_ref, wd_ref, g_ref, b_ref, o_ref, xb_sc, acc_sc):
    f = pl.program_id(1)

    @pl.when(f == 0)
    def _():
        xb_sc[...] = x_ref[...].astype(_BF16)
        acc_sc[...] = jnp.zeros_like(acc_sc)

    hid = jnp.maximum(_dot(xb_sc[...], wu_ref[...]), 0.0)
    acc_sc[...] += _dot((hid * hid).astype(_BF16), wd_ref[...])

    @pl.when(f == pl.num_programs(1) - 1)
    def _():
        o_ref[...] = _layer_norm(DN_ALPHA * x_ref[...] + acc_sc[...], g_ref[...], b_ref[...])


def _mlp_ln(x, w_up, w_down, g, b, tm, tf=512):
    m = x.shape[0]
    return pl.pallas_call(
        _mlp_body,
        grid=(m // tm, D_FF // tf),
        in_specs=[pl.BlockSpec((tm, D_MODEL), lambda i, f: (i, 0)),
                  pl.BlockSpec((D_MODEL, tf), lambda i, f: (0, f)),
                  pl.BlockSpec((tf, D_MODEL), lambda i, f: (f, 0)),
                  pl.BlockSpec((1, D_MODEL), lambda i, f: (0, 0)),
                  pl.BlockSpec((1, D_MODEL), lambda i, f: (0, 0))],
        out_specs=pl.BlockSpec((tm, D_MODEL), lambda i, f: (i, 0)),
        out_shape=jax.ShapeDtypeStruct((m, D_MODEL), _F32),
        scratch_shapes=[pltpu.VMEM((tm, D_MODEL), _BF16),
                        pltpu.VMEM((tm, D_MODEL), _F32)],
        compiler_params=_params("parallel", "arbitrary"),
        name="mlp_ln",
    )(x, w_up, w_down, g, b)


def _finish(x, r, od, gates, w, tm_merge, tm):
    merged = _merge(r, od, w["ret_out"], w["diff_out"], gates, tm_merge)
    x1 = _outproj_ln(merged, w["out"], x, w["ln1_g"], w["ln1_b"], tm)
    return _mlp_ln(x1, w["up"], w["down"], w["ln2_g"], w["ln2_b"], tm)


def kernel(x_prompt, x_sample, cache_k, cache_v, state_ret, page_table, w_in, w_ret_out, w_diff_out, w_out, lambda_q1, lambda_k1, lambda_q2, lambda_k2, subln_g, ln1_g, ln1_b, ln2_g, ln2_b, w_up, w_down):
    assert w_in.shape[0] == 1, "single-layer step"
    seq = x_prompt.shape[1]
    dec_b = x_sample.shape[0]
    n_pool = cache_k.shape[1]
    w_in0 = w_in[0]
    w = {"ret_out": w_ret_out[0].astype(_BF16), "diff_out": w_diff_out[0].astype(_BF16),
         "out": w_out[0].astype(_BF16),
         "ln1_g": ln1_g, "ln1_b": ln1_b, "ln2_g": ln2_g, "ln2_b": ln2_b,
         "up": w_up[0].astype(_BF16), "down": w_down[0].astype(_BF16)}
    heads = jnp.arange(N_HEADS, dtype=_F32)
    log_gamma = jnp.log1p(-jnp.exp2(-5.0 - heads)).reshape(N_HEADS, 1, 1)
    slopes = jnp.exp2(-8.0 * (heads + 1.0) / N_HEADS)
    lamv = jnp.concatenate([lambda_q1, lambda_k1, lambda_q2, lambda_k2], axis=0)
    subg = subln_g.reshape(1, DIFF_DV)

    xp = x_prompt[0]
    xp16 = xp.astype(_BF16)
    (ret_qkv,) = _proj(xp16, w_in0, COL_RET_QKV, 4096, [_BF16])
    (ret_gate,) = _proj(xp16, w_in0, COL_RET_GATE, 2048, [_F32])
    (dq,) = _proj(xp16, w_in0, COL_DIFF_Q, 2048, [_BF16], scale=DIFF_QSCALE * LOG2_E)
    k_rows, dk16 = _proj_k(xp16, w_in0, COL_DIFF_K)
    dv, dv16 = _proj(xp16, w_in0, COL_DIFF_V, 2048, [_F32, _BF16])
    (gates,) = _proj(xp16, w_in0, COL_MERGE_GATES, 4096, [_F32])
    r_p, st_p = _retention_prompt(log_gamma, ret_qkv, ret_gate)
    od_p = _diff_prompt((slopes * LOG2_E).reshape(N_HEADS, 1, 1), dq, dk16, dv16, lamv, subg)
    y_p = _finish(xp, r_p, od_p, gates, w, tm_merge=1024, tm=512)

    xs = x_sample[:, 0]
    xs16 = xs.astype(_BF16)
    (ret_s,) = _proj(xs16, w_in0, COL_RET_QKV, 6144, [_F32])
    (dq_s,) = _proj(xs16, w_in0, COL_DIFF_Q, 2048, [_F32], scale=DIFF_QSCALE)
    (rest_s,) = _proj(xs16, w_in0, COL_DIFF_K, 8192, [_F32])
    dk_s = rest_s[:, 0:2048]
    dv_s = rest_s[:, 2048:4096]
    gates_s = rest_s[:, 4096:8192]
    r_s, st_s = _retention_sample(
        log_gamma,
        ret_s[:, 0:1024].reshape(dec_b, N_HEADS, RET_DK),
        ret_s[:, 1024:2048].reshape(dec_b, N_HEADS, RET_DK),
        ret_s[:, 2048:4096].reshape(dec_b, N_HEADS, RET_DV),
        ret_s[:, 4096:6144].reshape(dec_b, N_HEADS, RET_DV),
        state_ret[0])
    od_s = _diff_sample(
        page_table, dq_s, dk_s, dv_s, slopes, lamv, subg,
        cache_k.reshape(n_pool, PAGE_SIZE * K_ROWS, DIFF_DH),
        cache_v.reshape(n_pool, PAGE_SIZE * V_ROWS, DIFF_DV))
    y_s = _finish(xs, r_s.reshape(dec_b, D_MODEL).astype(_BF16), od_s.reshape(dec_b, D_MODEL).astype(_BF16),
                  gates_s, w, tm_merge=dec_b, tm=dec_b)

    return (y_p[None], y_s[:, None, :],
            k_rows.reshape(1, 1, seq, N_HEADS, 2, DIFF_DH), dv.reshape(1, 1, seq, N_HEADS, DIFF_DV),
            st_p[None, None],
            dk_s.reshape(1, dec_b, 1, N_HEADS, 2, DIFF_DH), dv_s.reshape(1, dec_b, 1, N_HEADS, DIFF_DV),
            st_s[None])
```

```python
import functools
import math

import jax
import jax.numpy as jnp
import numpy as np
from jax import lax
from jax.experimental import pallas as pl
from jax.experimental.pallas import tpu as pltpu

D_MODEL = 2048
N_HEADS = 8
RET_DK = 128
RET_DV = 256
DIFF_DH = 128
DIFF_DV = 256
D_FF = 4 * D_MODEL
PAGE_SIZE = 128
DN_ALPHA = 2.0 ** 0.25
LAM_INIT = 0.8 - 0.6 * math.exp(-0.3 * 0)
LN_EPS = 1e-5
NORM_EPS = 1e-5
RET_KSCALE = RET_DK ** -0.5
DIFF_QSCALE = DIFF_DH ** -0.5

COL_RET_QKV = 0
COL_RET_GATE = 4096
COL_DIFF_Q = 6144
COL_DIFF_K = 8192
COL_DIFF_V = 10240
COL_MERGE_GATES = 12288

V7X_VMEM_LIMIT_BYTES = 56 * 1024 * 1024
LANES = 128
SUBLANES = 8

RET_CHUNK = 256
FLASH_BQ = 1024
FLASH_BK = 512
FLASH_ROW_CHUNK = 32
LOG2_E = math.log2(math.e)
DEC_PAGES_PER_STEP = 8

K_ROWS = 2 * N_HEADS
V_ROWS = N_HEADS

_NT = (((1,), (1,)), ((), ()))
_TN = (((0,), (0,)), ((), ()))
_BF16 = jnp.bfloat16
_F32 = jnp.float32


def _params(*semantics):
    return pltpu.CompilerParams(dimension_semantics=semantics,
                                vmem_limit_bytes=V7X_VMEM_LIMIT_BYTES)


def _dot(a, b):
    return jnp.dot(a, b, preferred_element_type=_F32)


def _layer_norm(y, g, b):
    mu = jnp.mean(y, -1, keepdims=True)
    yc = y - mu
    var = jnp.mean(yc * yc, -1, keepdims=True)
    return yc * lax.rsqrt(var + LN_EPS) * g + b


def _rms(y):
    return y * lax.rsqrt(jnp.mean(y * y, -1, keepdims=True) + NORM_EPS)


def _silu(g):
    return g * jax.nn.sigmoid(g)


def _diff_lambda(lam_ref):
    v = lam_ref[...]
    a = jnp.sum(v[0:1] * v[1:2], axis=-1, keepdims=True)
    b = jnp.sum(v[2:3] * v[3:4], axis=-1, keepdims=True)
    return jnp.exp(a) - jnp.exp(b) + LAM_INIT


def _sub_ln(o1, o2, lam_ref, g_ref):
    od = o1 - _diff_lambda(lam_ref) * o2
    return _rms(od) * g_ref[...] * (1.0 - LAM_INIT)


def _proj_body(x_ref, w_ref, *o_refs, scale):
    acc = _dot(x_ref[...], w_ref[...].astype(_BF16))
    if scale is not None:
        acc = acc * scale
    for o_ref in o_refs:
        o_ref[...] = acc.astype(o_ref.dtype)


def _proj(x, w, col0, ncols, out_dtypes, scale=None, tn=512):
    m, kd = x.shape
    tm = min(2048, m)
    c0 = col0 // tn
    return pl.pallas_call(
        functools.partial(_proj_body, scale=scale),
        grid=(m // tm, ncols // tn),
        in_specs=[pl.BlockSpec((tm, kd), lambda i, j: (i, 0)),
                  pl.BlockSpec((kd, tn), lambda i, j: (0, c0 + j))],
        out_specs=[pl.BlockSpec((tm, tn), lambda i, j: (i, j)) for _ in out_dtypes],
        out_shape=[jax.ShapeDtypeStruct((m, ncols), dt) for dt in out_dtypes],
        compiler_params=_params("parallel", "arbitrary"),
        name="proj",
    )(x, w)


def _proj_k_body(x_ref, w_ref, cache_ref, o16_ref, *, tm):
    h = pl.program_id(1)
    acc = _dot(x_ref[...], w_ref[...].astype(_BF16))
    o16_ref[...] = acc.astype(_BF16)
    for c in range(2):
        cache_ref[pl.ds(h * 2 + c, tm, stride=K_ROWS), :] = acc[:, c * DIFF_DH:(c + 1) * DIFF_DH]


def _proj_k(x, w, col0, tm=1024):
    m, kd = x.shape
    c0 = col0 // 256
    return pl.pallas_call(
        functools.partial(_proj_k_body, tm=tm),
        grid=(m // tm, N_HEADS),
        in_specs=[pl.BlockSpec((tm, kd), lambda i, h: (i, 0)),
                  pl.BlockSpec((kd, 256), lambda i, h: (0, c0 + h))],
        out_specs=[pl.BlockSpec((tm * K_ROWS, DIFF_DH), lambda i, h: (i, 0)),
                   pl.BlockSpec((tm, 256), lambda i, h: (i, h))],
        out_shape=[jax.ShapeDtypeStruct((m * K_ROWS, DIFF_DH), _F32),
                   jax.ShapeDtypeStruct((m, D_MODEL), _BF16)],
        compiler_params=_params("parallel", "arbitrary"),
        name="proj_k",
    )(x, w)


def _ret_body(lg_ref, q_ref, k_ref, v_ref, g_ref, r_ref, st_ref, state, dec, rowsc, colsc, *, chunk):
    c = pl.program_id(0)

    @pl.when(c == 0)
    def _():
        state[...] = jnp.zeros_like(state)
        i = lax.broadcasted_iota(jnp.int32, (chunk, chunk), 0)
        j = lax.broadcasted_iota(jnp.int32, (chunk, chunk), 1)
        rel = (i - j).astype(_F32)
        pos = lax.broadcasted_iota(jnp.int32, (chunk, 1), 0).astype(_F32)
        for h in range(N_HEADS):
            lg = lg_ref[h]
            dec[h] = jnp.where(rel >= 0, jnp.exp(lg * jnp.maximum(rel, 0.0)), 0.0) * RET_KSCALE
            rowsc[h] = jnp.exp(lg * (pos + 1.0))
            colsc[h] = jnp.exp(lg * (chunk - 1.0 - pos)) * RET_KSCALE

    for h in range(N_HEADS):
        q = q_ref[:, h * RET_DK:(h + 1) * RET_DK]
        k = k_ref[:, h * RET_DK:(h + 1) * RET_DK]
        v = v_ref[:, h * RET_DV:(h + 1) * RET_DV]
        s = lax.dot_general(q, k, _NT, preferred_element_type=_F32) * dec[h]
        st_h = state[h]
        o = _dot(s.astype(_BF16), v) + _dot(q, st_h.astype(_BF16)) * rowsc[h]
        kd = (k.astype(_F32) * colsc[h]).astype(_BF16)
        state[h] = (jnp.exp(lg_ref[h] * float(chunk)) * st_h
                    + lax.dot_general(kd, v, _TN, preferred_element_type=_F32))
        g = g_ref[:, h * RET_DV:(h + 1) * RET_DV]
        r_ref[:, h * RET_DV:(h + 1) * RET_DV] = (_rms(o) * _silu(g)).astype(r_ref.dtype)

    @pl.when(c == pl.num_programs(0) - 1)
    def _():
        st_ref[...] = state[...]


def _retention_prompt(lg, qkv, gate):
    s_len = qkv.shape[0]
    c = RET_CHUNK
    return pl.pallas_call(
        functools.partial(_ret_body, chunk=c),
        grid=(s_len // c,),
        in_specs=[pl.BlockSpec((N_HEADS, 1, 1), lambda i: (0, 0, 0)),
                  pl.BlockSpec((c, 1024), lambda i: (i, 0)),
                  pl.BlockSpec((c, 1024), lambda i: (i, 1)),
                  pl.BlockSpec((c, 2048), lambda i: (i, 1)),
                  pl.BlockSpec((c, 2048), lambda i: (i, 0))],
        out_specs=[pl.BlockSpec((c, 2048), lambda i: (i, 0)),
                   pl.BlockSpec((N_HEADS, RET_DK, RET_DV), lambda i: (0, 0, 0))],
        out_shape=[jax.ShapeDtypeStruct((s_len, 2048), _BF16),
                   jax.ShapeDtypeStruct((N_HEADS, RET_DK, RET_DV), _F32)],
        scratch_shapes=[pltpu.VMEM((N_HEADS, RET_DK, RET_DV), _F32),
                        pltpu.VMEM((N_HEADS, c, c), _F32),
                        pltpu.VMEM((N_HEADS, c, 1), _F32),
                        pltpu.VMEM((N_HEADS, c, 1), _F32)],
        compiler_params=_params("arbitrary"),
        name="retention_prompt",
    )(lg, qkv, qkv, qkv, gate)


def _ret_sample_body(lg_ref, q_ref, k_ref, v_ref, g_ref, st_ref, r_ref, nst_ref):
    eye = (lax.broadcasted_iota(jnp.int32, (RET_DK, RET_DK), 0)
           == lax.broadcasted_iota(jnp.int32, (RET_DK, RET_DK), 1))

    def column(row):
        return jnp.sum(jnp.where(eye, row, 0.0), axis=1, keepdims=True)

    for h in range(N_HEADS):
        qh = q_ref[0, h:h + 1, :]
        kh = k_ref[0, h:h + 1, :] * RET_KSCALE
        vh = v_ref[0, h:h + 1, :]
        st = st_ref[0, h]
        gam = jnp.exp(lg_ref[h])
        cross = jnp.sum(column(qh) * st, axis=0, keepdims=True) * gam
        qk = jnp.sum(qh * kh, axis=1, keepdims=True)
        o = qk * vh + cross
        nst_ref[0, h] = gam * st + column(kh) * vh
        r_ref[0, h:h + 1, :] = _rms(o) * _silu(g_ref[0, h:h + 1, :])


def _retention_sample(lg, q, k, v, gate, state):
    b = q.shape[0]
    vec = lambda d: pl.BlockSpec((1, N_HEADS, d), lambda i: (i, 0, 0))
    st_spec = pl.BlockSpec((1, N_HEADS, RET_DK, RET_DV), lambda i: (i, 0, 0, 0))
    return pl.pallas_call(
        _ret_sample_body,
        grid=(b,),
        in_specs=[pl.BlockSpec((N_HEADS, 1, 1), lambda i: (0, 0, 0)),
                  vec(RET_DK), vec(RET_DK), vec(RET_DV), vec(RET_DV), st_spec],
        out_specs=[vec(RET_DV), st_spec],
        out_shape=[jax.ShapeDtypeStruct((b, N_HEADS, RET_DV), _F32),
                   jax.ShapeDtypeStruct(state.shape, _F32)],
        compiler_params=_params("parallel"),
        name="retention_sample",
    )(lg, q, k, v, gate, state)


def _flash_init(m_sc, l_sc, acc_sc):
    m_sc[...] = jnp.full(m_sc.shape, -1e30, _F32)
    l_sc[...] = jnp.zeros_like(l_sc)
    acc_sc[...] = jnp.zeros_like(acc_sc)


def _flash_update(qi, kj, slope_ref, q_ref, k_ref, v_ref, m_sc, l_sc, acc_sc, s_sc, p_sc, a_sc, *, bq, bk, rc):
    col = lax.broadcasted_iota(jnp.int32, (1, bk), 1) + (kj * bk - qi * bq)
    bias = slope_ref[0] * col.astype(_F32)
    n_rep = bk // LANES
    v = v_ref[...]
    for c in range(2):
        q = q_ref[:, c * DIFF_DH:(c + 1) * DIFF_DH]
        k = k_ref[:, c * DIFF_DH:(c + 1) * DIFF_DH]
        s_sc[c] = lax.dot_general(q, k, _NT, preferred_element_type=_F32)
    for c in range(2):
        for r in range(bq // rc):
            rows = slice(r * rc, (r + 1) * rc)
            rowi = lax.broadcasted_iota(jnp.int32, (rc, bk), 0) + (qi * bq + r * rc)
            colp = lax.broadcasted_iota(jnp.int32, (rc, bk), 1) + kj * bk
            s = jnp.where(colp <= rowi, s_sc[c, rows, :] + bias, -jnp.inf)
            m_old = m_sc[c, rows, :]
            m_new = jnp.maximum(m_old, jnp.max(s, axis=-1, keepdims=True))
            alpha = jnp.exp2(m_old - m_new)
            p = jnp.exp2(s - jnp.concatenate([m_new] * n_rep, axis=1))
            l_sc[c, rows, :] = alpha * l_sc[c, rows, :] + jnp.sum(p, axis=-1, keepdims=True)
            m_sc[c, rows, :] = m_new
            a_sc[c, rows, :] = alpha
            p_sc[c, rows, :] = p.astype(_BF16)
        a = a_sc[c]
        acc_sc[c] = jnp.concatenate([a, a], axis=1) * acc_sc[c] + _dot(p_sc[c], v)


def _flash_finish(lam_ref, g_ref, o_ref, l_sc, acc_sc):
    l0 = l_sc[0]
    l1 = l_sc[1]
    o1 = acc_sc[0] / jnp.concatenate([l0, l0], axis=1)
    o2 = acc_sc[1] / jnp.concatenate([l1, l1], axis=1)
    o_ref[...] = _sub_ln(o1, o2, lam_ref, g_ref).astype(o_ref.dtype)


def _group_reduce(x, op, reduce_rows):
    r = x[:, 0:LANES]
    for t in range(1, x.shape[1] // LANES):
        r = op(r, x[:, t * LANES:(t + 1) * LANES])
    r = jnp.broadcast_to(reduce_rows(r, axis=0, keepdims=True), (SUBLANES, LANES))
    for shift in (K_ROWS, 2 * K_ROWS, 4 * K_ROWS):
        r = op(r, pltpu.roll(r, shift, axis=1))
    return r


def _pattern_to_column(pat):
    r = lax.broadcasted_iota(jnp.int32, (K_ROWS, LANES), 0)
    lane = lax.broadcasted_iota(jnp.int32, (K_ROWS, LANES), 1)
    own = lane == (r % N_HEADS) * 2 + r // N_HEADS
    pat16 = jnp.concatenate([pat, pat], axis=0)
    return jnp.sum(jnp.where(own, pat16, 0.0), axis=1, keepdims=True)


def _decode_update(j, q_ref, ba_ref, bb_ref, sel_ref, k_refs, v_refs, m_sc, l_sc, acc_sc, *, past_len):
    pages = len(k_refs)
    page_rows = PAGE_SIZE * K_ROWS
    own = (lax.broadcasted_iota(jnp.int32, (K_ROWS, page_rows), 1) % K_ROWS
           == lax.broadcasted_iota(jnp.int32, (K_ROWS, page_rows), 0))
    q = q_ref[0]
    rows = []
    for p in range(pages):
        g = lax.dot_general(q, k_refs[p][0], _NT, preferred_element_type=_F32)
        rows.append(jnp.sum(jnp.where(own, g, 0.0), axis=0, keepdims=True))
    s = jnp.concatenate(rows, axis=0)
    first_key = (past_len - j * (pages * PAGE_SIZE)).astype(_F32)
    s = s + (bb_ref[...] - ba_ref[...] * first_key)

    m_old = m_sc[...]
    m_new = jnp.maximum(m_old, _group_reduce(s, jnp.maximum, jnp.max))
    alpha = jnp.exp(m_old - m_new)
    p_all = jnp.exp(s - jnp.concatenate([m_new] * (page_rows // LANES), axis=1))
    l_sc[...] = alpha * l_sc[...] + _group_reduce(p_all, jnp.add, jnp.sum)
    m_sc[...] = m_new

    n_blk = page_rows // 256
    stacked = jnp.concatenate([p_all[:, b * 256:(b + 1) * 256] for b in range(n_blk)], axis=0)
    o = _dot(stacked.astype(_BF16), sel_ref[...])
    head_of_lane = (lax.broadcasted_iota(jnp.int32, (N_HEADS, LANES), 1) % N_HEADS
                    == lax.broadcasted_iota(jnp.int32, (N_HEADS, LANES), 0))
    acc = _pattern_to_column(alpha) * acc_sc[...]
    for p in range(pages):
        halves = []
        for c in range(2):
            blocks = []
            for b in range(n_blk):
                piece = o[b * pages + p:b * pages + p + 1, c * LANES:(c + 1) * LANES]
                blocks.append(jnp.where(head_of_lane, jnp.broadcast_to(piece, (N_HEADS, LANES)), 0.0))
            halves.append(jnp.concatenate(blocks, axis=1))
        lhs = jnp.concatenate(halves, axis=0).astype(_BF16)
        acc = acc + _dot(lhs, v_refs[p][0])
    acc_sc[...] = acc


def _decode_finish(qc_ref, kc_ref, vc_ref, lam_ref, g_ref, o_ref, m_sc, l_sc, acc_sc):
    m_col = _pattern_to_column(m_sc[...])
    l_col = _pattern_to_column(l_sc[...])
    s_new = jnp.sum(qc_ref[0] * kc_ref[0], axis=1, keepdims=True)
    m_fin = jnp.maximum(m_col, s_new)
    a = jnp.exp(m_col - m_fin)
    p_new = jnp.exp(s_new - m_fin)
    l_fin = a * l_col + p_new
    out = (a * acc_sc[...] + p_new * vc_ref[0]) / l_fin
    o_ref[0] = _sub_ln(out[0:N_HEADS], out[N_HEADS:], lam_ref, g_ref)


def _attn_body(pt_ref, hd_ref, qi_ref, kj_ref,
               slope_ref, fq_ref, fk_ref, fv_ref, lam_ref, g_ref,
               dq_ref, qc_ref, kc_ref, vc_ref, ba_ref, bb_ref, sel_ref, *rest,
               pages, chunks_per_row, dec_steps, past_len, bq, bk, rc):
    k_refs = rest[:pages]
    v_refs = rest[pages:2 * pages]
    fo_ref, do_ref = rest[2 * pages:2 * pages + 2]
    fm_sc, fl_sc, facc_sc, s_sc, p_sc, a_sc, dm_sc, dl_sc, dacc_sc = rest[2 * pages + 2:]
    t = pl.program_id(0)
    qi = qi_ref[t]
    kj = kj_ref[t]
    decoding = t < dec_steps
    j = t % chunks_per_row

    pl.when(kj == 0)(lambda: _flash_init(fm_sc, fl_sc, facc_sc))
    pl.when(jnp.logical_and(decoding, j == 0))(lambda: _flash_init(dm_sc, dl_sc, dacc_sc))

    def step(with_decode):
        if with_decode:
            _decode_update(j, dq_ref, ba_ref, bb_ref, sel_ref, k_refs, v_refs, dm_sc, dl_sc, dacc_sc,
                           past_len=past_len)
        _flash_update(qi, kj, slope_ref, fq_ref, fk_ref, fv_ref, fm_sc, fl_sc, facc_sc, s_sc, p_sc, a_sc,
                      bq=bq, bk=bk, rc=rc)

    pl.when(decoding)(lambda: step(True))
    pl.when(jnp.logical_not(decoding))(lambda: step(False))

    pl.when(kj == ((qi + 1) * bq - 1) // bk)(lambda: _flash_finish(lam_ref, g_ref, fo_ref, fl_sc, facc_sc))
    pl.when(jnp.logical_and(decoding, j == chunks_per_row - 1))(
        lambda: _decode_finish(qc_ref, kc_ref, vc_ref, lam_ref, g_ref, do_ref, dm_sc, dl_sc, dacc_sc))


def _map_selector():
    sel = np.zeros((256, 256), np.float32)
    for key in range(16):
        for h in range(N_HEADS):
            for c in range(2):
                sel[key * K_ROWS + h * 2 + c, c * LANES + key * N_HEADS + h] = 1.0
    return jnp.asarray(sel, _BF16)


def _diff_attention(slopes, lamv, subg, fq, fk, fv, page_table, q, k_new, v_new, cache_k, cache_v):
    s_len = fq.shape[0]
    b, n_pages = page_table.shape
    bq, bk, rc = FLASH_BQ, FLASH_BK, FLASH_ROW_CHUNK
    pages = DEC_PAGES_PER_STEP
    chunks_per_row = n_pages // pages
    dec_steps = b * chunks_per_row
    assert pages == SUBLANES and n_pages % pages == 0

    steps = [(h, i, j) for h in range(N_HEADS) for i in range(s_len // bq)
             for j in range(((i + 1) * bq - 1) // bk + 1)]
    assert len(steps) >= dec_steps
    hd_tab, qi_tab, kj_tab = (jnp.asarray(np.array([s[n] for s in steps], np.int32)) for n in range(3))

    def map_major(a):
        return a.reshape(b, N_HEADS, 2, DIFF_DH).transpose(0, 2, 1, 3).reshape(b, K_ROWS, DIFF_DH)

    q16 = q.reshape(b, K_ROWS, DIFF_DH).astype(_BF16)
    v_rows = jnp.tile(v_new.reshape(b, N_HEADS, DIFF_DV), (1, 2, 1))
    lane_slope = jnp.tile(jnp.repeat(slopes, 2), PAGE_SIZE)
    key_of_lane = jnp.repeat(jnp.arange(PAGE_SIZE, dtype=_F32), K_ROWS)
    page_of_row = jnp.arange(pages, dtype=_F32)[:, None] * PAGE_SIZE
    bias_a = jnp.broadcast_to(lane_slope[None, :], (pages, PAGE_SIZE * K_ROWS))
    bias_b = lane_slope[None, :] * (page_of_row + key_of_lane[None, :])

    last = dec_steps - 1
    row_of = lambda t: jnp.minimum(t, last) // chunks_per_row
    chunk_of = lambda t: jnp.minimum(t, last) % chunks_per_row
    const = lambda shape: pl.BlockSpec(shape, lambda t, pt, hd, qi, kj: (0,) * len(shape))
    vec = lambda rows, d: pl.BlockSpec((1, rows, d), lambda t, pt, hd, qi, kj: (row_of(t), 0, 0))

    def page_spec(p, rows, d):
        return pl.BlockSpec((1, rows, d),
                            lambda t, pt, hd, qi, kj: (pt[row_of(t), chunk_of(t) * pages + p], 0, 0))

    grid_spec = pltpu.PrefetchScalarGridSpec(
        num_scalar_prefetch=4,
        grid=(len(steps),),
        in_specs=([pl.BlockSpec((1, 1, 1), lambda t, pt, hd, qi, kj: (hd[t], 0, 0)),
                   pl.BlockSpec((bq, 256), lambda t, pt, hd, qi, kj: (qi[t], hd[t])),
                   pl.BlockSpec((bk, 256), lambda t, pt, hd, qi, kj: (kj[t], hd[t])),
                   pl.BlockSpec((bk, 256), lambda t, pt, hd, qi, kj: (kj[t], hd[t])),
                   const((4, DIFF_DH)), const((1, DIFF_DV)),
                   vec(K_ROWS, DIFF_DH), vec(K_ROWS, DIFF_DH), vec(K_ROWS, DIFF_DH), vec(K_ROWS, DIFF_DV),
                   const(bias_a.shape), const(bias_b.shape), const((256, 256))]
                  + [page_spec(p, PAGE_SIZE * K_ROWS, DIFF_DH) for p in range(pages)]
                  + [page_spec(p, PAGE_SIZE * V_ROWS, DIFF_DV) for p in range(pages)]),
        out_specs=[pl.BlockSpec((bq, 256), lambda t, pt, hd, qi, kj: (qi[t], hd[t])),
                   pl.BlockSpec((1, N_HEADS, DIFF_DV), lambda t, pt, hd, qi, kj: (row_of(t), 0, 0))],
        scratch_shapes=[pltpu.VMEM((2, bq, LANES), _F32),
                        pltpu.VMEM((2, bq, LANES), _F32),
                        pltpu.VMEM((2, bq, DIFF_DV), _F32),
                        pltpu.VMEM((2, bq, bk), _F32),
                        pltpu.VMEM((2, bq, bk), _BF16),
                        pltpu.VMEM((2, bq, LANES), _F32),
                        pltpu.VMEM((SUBLANES, LANES), _F32),
                        pltpu.VMEM((SUBLANES, LANES), _F32),
                        pltpu.VMEM((K_ROWS, DIFF_DV), _F32)],
    )
    return pl.pallas_call(
        functools.partial(_attn_body, pages=pages, chunks_per_row=chunks_per_row, dec_steps=dec_steps,
                          past_len=n_pages * PAGE_SIZE, bq=bq, bk=bk, rc=rc),
        grid_spec=grid_spec,
        out_shape=[jax.ShapeDtypeStruct((s_len, 2048), _BF16),
                   jax.ShapeDtypeStruct((b, N_HEADS, DIFF_DV), _F32)],
        compiler_params=_params("arbitrary"),
        name="diff_attention",
    )(page_table, hd_tab, qi_tab, kj_tab,
      (slopes * LOG2_E).reshape(N_HEADS, 1, 1), fq, fk, fv, lamv, subg,
      q16, map_major(q), map_major(k_new), v_rows, bias_a, bias_b, _map_selector(),
      *([cache_k] * pages), *([cache_v] * pages))


def _merge_body(r_ref, od_ref, wr_ref, wd_ref, ga_ref, gb_ref, o_ref):
    a = _dot(r_ref[...], wr_ref[...])
    b = _dot(od_ref[...], wd_ref[...])
    o_ref[...] = (jax.nn.sigmoid(ga_ref[...]) * a + jax.nn.sigmoid(gb_ref[...]) * b).astype(o_ref.dtype)


def _merge(r, od, w_ret_out, w_diff_out, gates, tm, tn=256):
    m = r.shape[0]
    nb = D_MODEL // tn
    return pl.pallas_call(
        _merge_body,
        grid=(m // tm, nb),
        in_specs=[pl.BlockSpec((tm, D_MODEL), lambda i, j: (i, 0)),
                  pl.BlockSpec((tm, D_MODEL), lambda i, j: (i, 0)),
                  pl.BlockSpec((D_MODEL, tn), lambda i, j: (0, j)),
                  pl.BlockSpec((D_MODEL, tn), lambda i, j: (0, j)),
                  pl.BlockSpec((tm, tn), lambda i, j: (i, j)),
                  pl.BlockSpec((tm, tn), lambda i, j: (i, nb + j))],
        out_specs=pl.BlockSpec((tm, tn), lambda i, j: (i, j)),
        out_shape=jax.ShapeDtypeStruct((m, D_MODEL), _BF16),
        compiler_params=_params("parallel", "arbitrary"),
        name="merge_gate",
    )(r, od, w_ret_out, w_diff_out, gates, gates)


def _outproj_body(a_ref, w_ref, x_ref, g_ref, b_ref, o_ref, acc_sc):
    k = pl.program_id(1)

    @pl.when(k == 0)
    def _():
        acc_sc[...] = jnp.zeros_like(acc_sc)

    acc_sc[...] += _dot(a_ref[...], w_ref[...])

    @pl.when(k == pl.num_programs(1) - 1)
    def _():
        o_ref[...] = _layer_norm(DN_ALPHA * x_ref[...] + acc_sc[...], g_ref[...], b_ref[...])


def _outproj_ln(a, w_out, x, g, b, tm, tk=512):
    m = a.shape[0]
    return pl.pallas_call(
        _outproj_body,
        grid=(m // tm, D_MODEL // tk),
        in_specs=[pl.BlockSpec((tm, tk), lambda i, k: (i, k)),
                  pl.BlockSpec((tk, D_MODEL), lambda i, k: (k, 0)),
                  pl.BlockSpec((tm, D_MODEL), lambda i, k: (i, 0)),
                  pl.BlockSpec((1, D_MODEL), lambda i, k: (0, 0)),
                  pl.BlockSpec((1, D_MODEL), lambda i, k: (0, 0))],
        out_specs=pl.BlockSpec((tm, D_MODEL), lambda i, k: (i, 0)),
        out_shape=jax.ShapeDtypeStruct((m, D_MODEL), _F32),
        scratch_shapes=[pltpu.VMEM((tm, D_MODEL), _F32)],
        compiler_params=_params("parallel", "arbitrary"),
        name="outproj_ln",
    )(a, w_out, x, g, b)


def _mlp_body(x_ref, wu_ref, wd_ref, g_ref, b_ref, o_ref, xb_sc, acc_sc):
    f = pl.program_id(1)

    @pl.when(f == 0)
    def _():
        xb_sc[...] = x_ref[...].astype(_BF16)
        acc_sc[...] = jnp.zeros_like(acc_sc)

    hid = jnp.maximum(_dot(xb_sc[...], wu_ref[...]), 0.0)
    acc_sc[...] += _dot((hid * hid).astype(_BF16), wd_ref[...])

    @pl.when(f == pl.num_programs(1) - 1)
    def _():
        o_ref[...] = _layer_norm(DN_ALPHA * x_ref[...] + acc_sc[...], g_ref[...], b_ref[...])


def _mlp_ln(x, w_up, w_down, g, b, tm, tf=512):
    m = x.shape[0]
    return pl.pallas_call(
        _mlp_body,
        grid=(m // tm, D_FF // tf),
        in_specs=[pl.BlockSpec((tm, D_MODEL), lambda i, f: (i, 0)),
                  pl.BlockSpec((D_MODEL, tf), lambda i, f: (0, f)),
                  pl.BlockSpec((tf, D_MODEL), lambda i, f: (f, 0)),
                  pl.BlockSpec((1, D_MODEL), lambda i, f: (0, 0)),
                  pl.BlockSpec((1, D_MODEL), lambda i, f: (0, 0))],
        out_specs=pl.BlockSpec((tm, D_MODEL), lambda i, f: (i, 0)),
        out_shape=jax.ShapeDtypeStruct((m, D_MODEL), _F32),
        scratch_shapes=[pltpu.VMEM((tm, D_MODEL), _BF16),
                        pltpu.VMEM((tm, D_MODEL), _F32)],
        compiler_params=_params("parallel", "arbitrary"),
        name="mlp_ln",
    )(x, w_up, w_down, g, b)


def _finish(x, r, od, gates, w, tm_merge, tm):
    merged = _merge(r, od, w["ret_out"], w["diff_out"], gates, tm_merge)
    x1 = _outproj_ln(merged, w["out"], x, w["ln1_g"], w["ln1_b"], tm)
    return _mlp_ln(x1, w["up"], w["down"], w["ln2_g"], w["ln2_b"], tm)


def kernel(x_prompt, x_sample, cache_k, cache_v, state_ret, page_table, w_in, w_ret_out, w_diff_out, w_out, lambda_q1, lambda_k1, lambda_q2, lambda_k2, subln_g, ln1_g, ln1_b, ln2_g, ln2_b, w_up, w_down):
    assert w_in.shape[0] == 1, "single-layer step"
    seq = x_prompt.shape[1]
    dec_b = x_sample.shape[0]
    n_pool = cache_k.shape[1]
    w_in0 = w_in[0]
    w = {"ret_out": w_ret_out[0].astype(_BF16), "diff_out": w_diff_out[0].astype(_BF16),
         "out": w_out[0].astype(_BF16),
         "ln1_g": ln1_g, "ln1_b": ln1_b, "ln2_g": ln2_g, "ln2_b": ln2_b,
         "up": w_up[0].astype(_BF16), "down": w_down[0].astype(_BF16)}
    heads = jnp.arange(N_HEADS, dtype=_F32)
    log_gamma = jnp.log1p(-jnp.exp2(-5.0 - heads)).reshape(N_HEADS, 1, 1)
    slopes = jnp.exp2(-8.0 * (heads + 1.0) / N_HEADS)
    lamv = jnp.concatenate([lambda_q1, lambda_k1, lambda_q2, lambda_k2], axis=0)
    subg = subln_g.reshape(1, DIFF_DV)

    xp = x_prompt[0]
    xp16 = xp.astype(_BF16)
    (ret_qkv,) = _proj(xp16, w_in0, COL_RET_QKV, 4096, [_BF16])
    (ret_gate,) = _proj(xp16, w_in0, COL_RET_GATE, 2048, [_F32])
    (dq,) = _proj(xp16, w_in0, COL_DIFF_Q, 2048, [_BF16], scale=DIFF_QSCALE * LOG2_E)
    k_rows, dk16 = _proj_k(xp16, w_in0, COL_DIFF_K)
    dv, dv16 = _proj(xp16, w_in0, COL_DIFF_V, 2048, [_F32, _BF16])
    (gates,) = _proj(xp16, w_in0, COL_MERGE_GATES, 4096, [_F32])
    r_p, st_p = _retention_prompt(log_gamma, ret_qkv, ret_gate)

    xs = x_sample[:, 0]
    xs16 = xs.astype(_BF16)
    (ret_s,) = _proj(xs16, w_in0, COL_RET_QKV, 6144, [_F32])
    (dq_s,) = _proj(xs16, w_in0, COL_DIFF_Q, 2048, [_F32], scale=DIFF_QSCALE)
    (rest_s,) = _proj(xs16, w_in0, COL_DIFF_K, 8192, [_F32])
    dk_s = rest_s[:, 0:2048]
    dv_s = rest_s[:, 2048:4096]
    gates_s = rest_s[:, 4096:8192]
    r_s, st_s = _retention_sample(
        log_gamma,
        ret_s[:, 0:1024].reshape(dec_b, N_HEADS, RET_DK),
        ret_s[:, 1024:2048].reshape(dec_b, N_HEADS, RET_DK),
        ret_s[:, 2048:4096].reshape(dec_b, N_HEADS, RET_DV),
        ret_s[:, 4096:6144].reshape(dec_b, N_HEADS, RET_DV),
        state_ret[0])

    od_p, od_s = _diff_attention(
        slopes, lamv, subg, dq, dk16, dv16, page_table, dq_s, dk_s, dv_s,
        cache_k.reshape(n_pool, PAGE_SIZE * K_ROWS, DIFF_DH),
        cache_v.reshape(n_pool, PAGE_SIZE * V_ROWS, DIFF_DV))
    y_p = _finish(xp, r_p, od_p, gates, w, tm_merge=1024, tm=512)
    y_s = _finish(xs, r_s.reshape(dec_b, D_MODEL).astype(_BF16), od_s.reshape(dec_b, D_MODEL).astype(_BF16),
                  gates_s, w, tm_merge=dec_b, tm=dec_b)

    return (y_p[None], y_s[:, None, :],
            k_rows.reshape(1, 1, seq, N_HEADS, 2, DIFF_DH), dv.reshape(1, 1, seq, N_HEADS, DIFF_DV),
            st_p[None, None],
            dk_s.reshape(1, dec_b, 1, N_HEADS, 2, DIFF_DH), dv_s.reshape(1, dec_b, 1, N_HEADS, DIFF_DV),
            st_s[None])
```
